```python
import jax, jax.numpy as jnp
from jax import lax
import numpy as np

D_MODEL = 2048
BATCH = 4
SEQ = 8192
DEPTH = 1

RET_HEADS = 8
RET_DK = 128
RET_DV = 256
RET_CHUNK = 128
ROPE_BASE = 10000.0
CONV_CH = 1024
CONV_K = 31
N_EXPERTS = 32
TOP_K = 4
D_FF = 2048
SWIGLU_LIMIT = 7.0
SWIGLU_ALPHA = 1.702
MOE_BLOCK = 256
LN_EPS = 1e-5
DEEPNORM_ALPHA = (2 * DEPTH) ** 0.25
DEEPNORM_BETA = (8 * DEPTH) ** -0.25

Q_W = RET_HEADS * RET_DK
V_W = RET_HEADS * RET_DV
IN_WIDTHS = (Q_W, Q_W, V_W, V_W, CONV_CH, CONV_CH, D_MODEL, D_MODEL)
IN_W = sum(IN_WIDTHS)
SPLIT_POINTS = [int(s) for s in np.cumsum(IN_WIDTHS)[:-1]]

kernel_name = "deepnorm_gated_retention_conformer_moe"


def _layernorm(x, g, b):
    xf = x.astype(jnp.float32)
    mu = jnp.mean(xf, axis=-1, keepdims=True)
    var = jnp.mean(jnp.square(xf - mu), axis=-1, keepdims=True)
    y = (xf - mu) * lax.rsqrt(var + LN_EPS)
    return (y * g.astype(jnp.float32) + b.astype(jnp.float32)).astype(x.dtype)


def _rotary(x):
    s, dh = x.shape[1], x.shape[-1]
    half = dh // 2
    inv_freq = ROPE_BASE ** (-jnp.arange(half, dtype=jnp.float32) / half)
    ang = jnp.arange(s, dtype=jnp.float32)[:, None] * inv_freq[None, :]
    cos, sin = jnp.cos(ang)[None, :, None, :], jnp.sin(ang)[None, :, None, :]
    x1, x2 = x[..., :half], x[..., half:]
    return jnp.concatenate([x1 * cos - x2 * sin, x1 * sin + x2 * cos], axis=-1)


def _retention(q, k, v, g, gn_g, gn_b):
    bsz, s, _ = q.shape
    dt = v.dtype
    n_chunks = s // RET_CHUNK
    qf = _rotary(q.astype(jnp.float32).reshape(bsz, s, RET_HEADS, RET_DK))
    kf = _rotary(k.astype(jnp.float32).reshape(bsz, s, RET_HEADS, RET_DK)) * (RET_DK ** -0.5)
    vf = v.astype(jnp.float32).reshape(bsz, s, RET_HEADS, RET_DV)
    qc = qf.reshape(bsz, n_chunks, RET_CHUNK, RET_HEADS, RET_DK)
    kc = kf.reshape(bsz, n_chunks, RET_CHUNK, RET_HEADS, RET_DK)
    vc = vf.reshape(bsz, n_chunks, RET_CHUNK, RET_HEADS, RET_DV)

    log_gamma = jnp.log(1.0 - jnp.exp2(-5.0 - jnp.arange(RET_HEADS, dtype=jnp.float32)))
    idx = jnp.arange(RET_CHUNK, dtype=jnp.float32)
    diff = idx[:, None] - idx[None, :]
    decay_mask = jnp.where(diff[None] >= 0,
                           jnp.exp(jnp.maximum(diff, 0.0)[None] * log_gamma[:, None, None]), 0.0)
    k_decay = jnp.exp((RET_CHUNK - 1 - idx)[None, :] * log_gamma[:, None])
    q_decay = jnp.exp((idx + 1.0)[None, :] * log_gamma[:, None])
    chunk_decay = jnp.exp(RET_CHUNK * log_gamma)

    scores = jnp.einsum('bnihd,bnjhd->bnhij', qc, kc) * decay_mask
    intra = jnp.einsum('bnhij,bnjhv->bnihv', scores, vc)
    kv = jnp.einsum('bnjhd,hj,bnjhv->nbhdv', kc, k_decay, vc)

    def step(state, kv_n):
        return chunk_decay[None, :, None, None] * state + kv_n, state

    init = jnp.zeros((bsz, RET_HEADS, RET_DK, RET_DV), jnp.float32)
    _, prev_states = lax.scan(step, init, kv)
    cross = jnp.einsum('bnihd,hi,nbhdv->bnihv', qc, q_decay, prev_states)
    o = (intra + cross).reshape(bsz, s, RET_HEADS, RET_DV)
    o = _layernorm(o, gn_g.reshape(RET_HEADS, RET_DV), gn_b.reshape(RET_HEADS, RET_DV))
    o = o.reshape(bsz, s, V_W).astype(dt)
    return jax.nn.silu(g) * o


def _conformer_conv(c_val, c_gate, w_dw, b_dw, ln_g, ln_b):
    u = c_val * jax.nn.sigmoid(c_gate)
    y = lax.conv_general_dilated(
        u, w_dw.reshape(CONV_K, 1, CONV_CH).astype(u.dtype),
        window_strides=(1,), padding=[(CONV_K - 1, 0)],
        dimension_numbers=('NWC', 'WIO', 'NWC'), feature_group_count=CONV_CH) + b_dw
    return jax.nn.silu(_layernorm(y, ln_g, ln_b))


def _hybrid_mixer(x, w_in, w_dw, b_dw, conv_ln_g, conv_ln_b, w_conv_out,
                  ret_gn_g, ret_gn_b, w_ret_out, w_o):
    proj = x @ w_in
    q, k, v, g, c_val, c_gate, gate_c, gate_r = jnp.split(proj, SPLIT_POINTS, axis=-1)
    y_conv = _conformer_conv(c_val, c_gate, w_dw, b_dw, conv_ln_g, conv_ln_b) @ w_conv_out
    y_ret = _retention(q, k, v, g, ret_gn_g, ret_gn_b) @ w_ret_out
    h = jax.nn.sigmoid(gate_c) * y_conv + jax.nn.sigmoid(gate_r) * y_ret
    return h @ w_o


def _moe(x, w_router, b_router, w_gate_up, b_gate_up, w_down, b_down):
    bsz, s, d = x.shape
    t = bsz * s
    n_pairs = t * TOP_K
    xf = x.reshape(t, d)
    logits = (xf @ w_router + b_router).astype(jnp.float32)
    top_val, top_idx = lax.top_k(logits, TOP_K)
    gates = jax.nn.softmax(top_val, axis=-1)

    flat_e = top_idx.reshape(-1).astype(jnp.int32)
    flat_w = gates.reshape(-1)
    flat_tok = jnp.arange(n_pairs, dtype=jnp.int32) // TOP_K
    order = jnp.argsort(flat_e)
    se, stok, sw = flat_e[order], flat_tok[order], flat_w[order]
    counts = jnp.bincount(flat_e, length=N_EXPERTS)
    padded = ((counts + MOE_BLOCK - 1) // MOE_BLOCK) * MOE_BLOCK
    pad_end = jnp.cumsum(padded)
    pad_start = pad_end - padded
    start = jnp.cumsum(counts) - counts
    pos = pad_start[se] + (jnp.arange(n_pairs, dtype=jnp.int32) - start[se])

    n_pad = -(-n_pairs // MOE_BLOCK) * MOE_BLOCK + N_EXPERTS * MOE_BLOCK
    n_blocks = n_pad // MOE_BLOCK
    tok_pad = jnp.full((n_pad,), t, jnp.int32).at[pos].set(stok)
    w_pad = jnp.zeros((n_pad,), jnp.float32).at[pos].set(sw)
    block_e = jnp.minimum(
        jnp.searchsorted(pad_end, jnp.arange(n_blocks, dtype=pad_end.dtype) * MOE_BLOCK, side='right'),
        N_EXPERTS - 1)
    x_ext = jnp.concatenate([xf, jnp.zeros((1, d), xf.dtype)], axis=0)

    def expert_block(args):
        toks, e = args
        xb = x_ext[toks]
        gu = xb @ w_gate_up[e] + b_gate_up[e]
        gate, up = gu[:, :D_FF], gu[:, D_FF:]
        gate = jnp.minimum(gate, SWIGLU_LIMIT)
        up = jnp.clip(up, -SWIGLU_LIMIT, SWIGLU_LIMIT)
        hdn = (up + 1.0) * (gate * jax.nn.sigmoid(SWIGLU_ALPHA * gate))
        return hdn @ w_down[e] + b_down[e]

    yb = lax.map(expert_block, (tok_pad.reshape(n_blocks, MOE_BLOCK), block_e))
    y = jnp.zeros((t + 1, d), jnp.float32).at[tok_pad].add(
        yb.reshape(n_pad, d).astype(jnp.float32) * w_pad[:, None])
    return y[:t].reshape(bsz, s, d).astype(x.dtype)


def setup_inputs(seed: int = 0) -> dict:
    key = jax.random.key(seed)
    ks = jax.random.split(key, 22)
    f32 = jnp.float32

    def nrm(k, shape, scale):
        return jax.random.normal(k, shape, f32) * scale

    beta = DEEPNORM_BETA
    col_scale = jnp.concatenate([
        jnp.ones((2 * Q_W,), f32), jnp.full((V_W,), beta, f32), jnp.ones((V_W,), f32),
        jnp.full((CONV_CH,), beta, f32), jnp.ones((CONV_CH + 2 * D_MODEL,), f32)])
    return {
        "x": nrm(ks[0], (BATCH, SEQ, D_MODEL), 1.0),
        "w_in": nrm(ks[1], (DEPTH, D_MODEL, IN_W), D_MODEL ** -0.5) * col_scale,
        "w_dw": nrm(ks[2], (DEPTH, CONV_K, CONV_CH), CONV_K ** -0.5),
        "b_dw": nrm(ks[3], (DEPTH, CONV_CH), 0.02),
        "conv_ln_g": 1.0 + nrm(ks[4], (DEPTH, CONV_CH), 0.02),
        "conv_ln_b": nrm(ks[5], (DEPTH, CONV_CH), 0.02),
        "w_conv_out": nrm(ks[6], (DEPTH, CONV_CH, D_MODEL), beta * CONV_CH ** -0.5),
        "ret_gn_g": 1.0 + nrm(ks[7], (DEPTH, V_W), 0.02),
        "ret_gn_b": nrm(ks[8], (DEPTH, V_W), 0.02),
        "w_ret_out": nrm(ks[9], (DEPTH, V_W, D_MODEL), beta * V_W ** -0.5),
        "w_o": nrm(ks[10], (DEPTH, D_MODEL, D_MODEL), beta * D_MODEL ** -0.5),
        "ln1_g": 1.0 + nrm(ks[11], (DEPTH, D_MODEL), 0.02),
        "ln1_b": nrm(ks[12], (DEPTH, D_MODEL), 0.02),
        "w_router": nrm(ks[13], (DEPTH, D_MODEL, N_EXPERTS), D_MODEL ** -0.5),
        "b_router": nrm(ks[14], (DEPTH, N_EXPERTS), 0.01),
        "w_gate_up": nrm(ks[15], (DEPTH, N_EXPERTS, D_MODEL, 2 * D_FF), beta * D_MODEL ** -0.5),
        "b_gate_up": nrm(ks[16], (DEPTH, N_EXPERTS, 2 * D_FF), 0.02),
        "w_down": nrm(ks[17], (DEPTH, N_EXPERTS, D_FF, D_MODEL), beta * D_FF ** -0.5),
        "b_down": nrm(ks[18], (DEPTH, N_EXPERTS, D_MODEL), 0.02),
        "ln2_g": 1.0 + nrm(ks[19], (DEPTH, D_MODEL), 0.02),
        "ln2_b": nrm(ks[20], (DEPTH, D_MODEL), 0.02),
    }


def reference(x, w_in, w_dw, b_dw, conv_ln_g, conv_ln_b, w_conv_out, ret_gn_g, ret_gn_b,
              w_ret_out, w_o, ln1_g, ln1_b, w_router, b_router, w_gate_up, b_gate_up,
              w_down, b_down, ln2_g, ln2_b):
    for l in range(DEPTH):
        mix = _hybrid_mixer(x, w_in[l], w_dw[l], b_dw[l], conv_ln_g[l], conv_ln_b[l], w_conv_out[l],
                            ret_gn_g[l], ret_gn_b[l], w_ret_out[l], w_o[l])
        x = _layernorm(DEEPNORM_ALPHA * x + mix, ln1_g[l], ln1_b[l])
        ffn = _moe(x, w_router[l], b_router[l], w_gate_up[l], b_gate_up[l], w_down[l], b_down[l])
        x = _layernorm(DEEPNORM_ALPHA * x + ffn, ln2_g[l], ln2_b[l])
    return x
```

```python
import functools

import jax
import jax.numpy as jnp
import numpy as np
from jax import lax
from jax.experimental import pallas as pl
from jax.experimental.pallas import tpu as pltpu

RET_HEADS = 8
RET_DK = 128
RET_DV = 256
RET_CHUNK = 128
ROPE_BASE = 10000.0
CONV_K = 31
TOP_K = 4
SWIGLU_LIMIT = 7.0
SWIGLU_ALPHA = 1.702
LN_EPS = 1e-5

LANES = 128
CONV_HALO = 32
VMEM_LIMIT = 56 * 1024 * 1024

F32 = jnp.float32
BF16 = jnp.bfloat16


def _cparams(sem):
    return pltpu.CompilerParams(dimension_semantics=sem, vmem_limit_bytes=VMEM_LIMIT)


def _layernorm(x, g, b):
    mu = jnp.mean(x, axis=-1, keepdims=True)
    xc = x - mu
    var = jnp.mean(xc * xc, axis=-1, keepdims=True)
    return xc * lax.rsqrt(var + LN_EPS) * g + b


def _sigmoid(x):
    return 1.0 / (1.0 + jnp.exp(-x))


def _store_slabs(slab_ref, y):
    rows, d = y.shape
    s_per = d // LANES
    for s in range(s_per):
        slab_ref[pl.ds(s, rows, stride=s_per), :] = y[:, s * LANES:(s + 1) * LANES].astype(slab_ref.dtype)


def _load_slab_cols(slab_ref, s, rows, s_per):
    return slab_ref[pl.ds(s, rows, stride=s_per), :]


def _in_proj_kernel(x_ref, w_ref, cos_ref, sin_ref, o_ref, xb_ref, *, n_rope_tiles):
    j = pl.program_id(1)

    @pl.when(j == 0)
    def _():
        xb_ref[...] = x_ref[...].astype(BF16)

    acc = jnp.dot(xb_ref[...], w_ref[...], preferred_element_type=F32)

    @pl.when(j < n_rope_tiles)
    def _():
        cos = cos_ref[...]
        sin = sin_ref[...]
        for g in range(acc.shape[1] // LANES):
            a = acc[:, g * LANES:(g + 1) * LANES]
            r = a * cos + pltpu.roll(a, LANES // 2, 1) * sin
            o_ref[:, g * LANES:(g + 1) * LANES] = r.astype(o_ref.dtype)

    @pl.when(j >= n_rope_tiles)
    def _():
        o_ref[...] = acc.astype(o_ref.dtype)


def _in_proj(x2d, w_bf16, cos_t, sin_t, *, seq, tm, tn, rope_cols):
    t, d = x2d.shape
    n = w_bf16.shape[1]
    blocks_per_seq = seq // tm
    return pl.pallas_call(
        functools.partial(_in_proj_kernel, n_rope_tiles=rope_cols // tn),
        grid=(t // tm, n // tn),
        in_specs=[
            pl.BlockSpec((tm, d), lambda i, j: (i, 0)),
            pl.BlockSpec((d, tn), lambda i, j: (0, j)),
            pl.BlockSpec((tm, LANES), lambda i, j: (i % blocks_per_seq, 0)),
            pl.BlockSpec((tm, LANES), lambda i, j: (i % blocks_per_seq, 0)),
        ],
        out_specs=pl.BlockSpec((tm, tn), lambda i, j: (i, j)),
        out_shape=jax.ShapeDtypeStruct((t, n), BF16),
        scratch_shapes=[pltpu.VMEM((tm, d), BF16)],
        compiler_params=_cparams(("parallel", "arbitrary")),
        name="in_proj",
    )(x2d, w_bf16, cos_t, sin_t)


def _retention_kernel(q_ref, k_ref, v_ref, g_ref, mask_ref, qdec_ref, kdec_ref, cdec_ref,
                      gng_ref, gnb_ref, o_ref, state_ref):
    n = pl.program_id(1)

    @pl.when(n == 0)
    def _():
        state_ref[...] = jnp.zeros_like(state_ref)

    for h in range(RET_HEADS):
        ks = slice(h * RET_DK, (h + 1) * RET_DK)
        vs = slice(h * RET_DV, (h + 1) * RET_DV)
        qh = q_ref[:, ks]
        kh = k_ref[:, ks]
        vh = v_ref[:, vs]
        s = lax.dot_general(qh, kh, (((1,), (1,)), ((), ())), preferred_element_type=F32)
        s = s * mask_ref[h]
        intra = jnp.dot(s.astype(BF16), vh, preferred_element_type=F32)
        st = state_ref[h]
        cross = jnp.dot(qh, st.astype(BF16), preferred_element_type=F32) * qdec_ref[h]
        vd = (vh.astype(F32) * kdec_ref[h]).astype(BF16)
        kv = jnp.dot(kh.T, vd, preferred_element_type=F32)
        state_ref[h] = cdec_ref[h] * st + kv
        y = _layernorm(intra + cross, gng_ref[:, vs], gnb_ref[:, vs])
        gg = g_ref[:, vs].astype(F32)
        o_ref[:, vs] = (gg * _sigmoid(gg) * y).astype(o_ref.dtype)


def _retention(proj, tables, gn_g, gn_b, *, batch, seq):
    t = proj.shape[0]
    c = RET_CHUNK
    n_chunks = seq // c
    q_w = RET_HEADS * RET_DK
    v_w = RET_HEADS * RET_DV
    mask, qdec, kdec, cdec = tables
    row = lambda b, n: b * n_chunks + n
    const3 = lambda b, n: (0, 0, 0)
    const2 = lambda b, n: (0, 0)
    return pl.pallas_call(
        _retention_kernel,
        grid=(batch, n_chunks),
        in_specs=[
            pl.BlockSpec((c, q_w), lambda b, n: (row(b, n), 0)),
            pl.BlockSpec((c, q_w), lambda b, n: (row(b, n), 1)),
            pl.BlockSpec((c, v_w), lambda b, n: (row(b, n), 2 * q_w // v_w)),
            pl.BlockSpec((c, v_w), lambda b, n: (row(b, n), 2 * q_w // v_w + 1)),
            pl.BlockSpec(mask.shape, const3),
            pl.BlockSpec(qdec.shape, const3),
            pl.BlockSpec(kdec.shape, const3),
            pl.BlockSpec(cdec.shape, const3),
            pl.BlockSpec((1, v_w), const2),
            pl.BlockSpec((1, v_w), const2),
        ],
        out_specs=pl.BlockSpec((c, v_w), lambda b, n: (row(b, n), 0)),
        out_shape=jax.ShapeDtypeStruct((t, v_w), BF16),
        scratch_shapes=[pltpu.VMEM((RET_HEADS, RET_DK, RET_DV), F32)],
        compiler_params=_cparams(("parallel", "arbitrary")),
        name="retention",
    )(proj, proj, proj, proj, mask, qdec, kdec, cdec, gn_g, gn_b)


def _retention_tables():
    log_gamma = jnp.log(1.0 - jnp.exp2(-5.0 - jnp.arange(RET_HEADS, dtype=F32)))
    idx = jnp.arange(RET_CHUNK, dtype=F32)
    diff = idx[:, None] - idx[None, :]
    scale = RET_DK ** -0.5
    mask = jnp.where(diff[None] >= 0,
                     jnp.exp(jnp.maximum(diff, 0.0)[None] * log_gamma[:, None, None]), 0.0) * scale
    k_decay = jnp.exp((RET_CHUNK - 1 - idx)[None, :] * log_gamma[:, None]) * scale
    q_decay = jnp.exp((idx + 1.0)[None, :] * log_gamma[:, None])
    chunk_decay = jnp.exp(RET_CHUNK * log_gamma)
    bshape = (RET_HEADS, RET_CHUNK, RET_DV)
    qdec = jnp.broadcast_to(q_decay[:, :, None], bshape)
    kdec = jnp.broadcast_to(k_decay[:, :, None], bshape)
    cdec = jnp.broadcast_to(chunk_decay[:, None, None], (RET_HEADS, 1, RET_DV))
    return mask.astype(F32), qdec.astype(F32), kdec.astype(F32), cdec.astype(F32)


def _rope_tables(seq):
    half = RET_DK // 2
    inv_freq = ROPE_BASE ** (-jnp.arange(half, dtype=F32) / half)
    ang = jnp.arange(seq, dtype=F32)[:, None] * inv_freq[None, :]
    cos, sin = jnp.cos(ang), jnp.sin(ang)
    return jnp.concatenate([cos, cos], axis=-1), jnp.concatenate([-sin, sin], axis=-1)


def _conv_kernel(cv_ref, cg_ref, pv_ref, pg_ref, w_ref, b_ref, lng_ref, lnb_ref, o_ref, u_ref, *, rows):
    i = pl.program_id(1)
    tc = cv_ref.shape[0]
    cv = cv_ref[...].astype(F32)
    cg = cg_ref[...].astype(F32)
    pv = pv_ref[...].astype(F32)
    pg = pg_ref[...].astype(F32)
    prev = pv * _sigmoid(pg)
    u_ref[0:CONV_HALO, :] = jnp.where(i == 0, 0.0, prev)
    u_ref[CONV_HALO:CONV_HALO + tc, :] = cv * _sigmoid(cg)
    off = CONV_HALO - (CONV_K - 1)
    for r in range(tc // rows):
        base = r * rows
        acc = jnp.zeros((rows, cv_ref.shape[1]), F32)
        for j in range(CONV_K):
            acc = acc + w_ref[j:j + 1, :] * u_ref[base + off + j:base + off + j + rows, :]
        y = _layernorm(acc + b_ref[...], lng_ref[...], lnb_ref[...])
        o_ref[base:base + rows, :] = (y * _sigmoid(y)).astype(o_ref.dtype)


def _conv_branch(proj, w_dw, b_dw, ln_g, ln_b, *, batch, seq, tc, col0, rows=16):
    t = proj.shape[0]
    ch = w_dw.shape[1]
    nblk = seq // tc
    cv_col = col0 // ch
    halo_per_blk = tc // CONV_HALO
    cur = lambda b, i: b * nblk + i
    prev = lambda b, i: jnp.maximum((b * nblk + i) * halo_per_blk - 1, 0)
    const = lambda b, i: (0, 0)
    return pl.pallas_call(
        functools.partial(_conv_kernel, rows=rows),
        grid=(batch, nblk),
        in_specs=[
            pl.BlockSpec((tc, ch), lambda b, i: (cur(b, i), cv_col)),
            pl.BlockSpec((tc, ch), lambda b, i: (cur(b, i), cv_col + 1)),
            pl.BlockSpec((CONV_HALO, ch), lambda b, i: (prev(b, i), cv_col)),
            pl.BlockSpec((CONV_HALO, ch), lambda b, i: (prev(b, i), cv_col + 1)),
            pl.BlockSpec((CONV_K, ch), const),
            pl.BlockSpec((1, ch), const),
            pl.BlockSpec((1, ch), const),
            pl.BlockSpec((1, ch), const),
        ],
        out_specs=pl.BlockSpec((tc, ch), lambda b, i: (cur(b, i), 0)),
        out_shape=jax.ShapeDtypeStruct((t, ch), BF16),
        scratch_shapes=[pltpu.VMEM((CONV_HALO + tc, ch), F32)],
        compiler_params=_cparams(("parallel", "arbitrary")),
        name="conv_branch",
    )(proj, proj, proj, proj, w_dw, b_dw, ln_g, ln_b)


def _merge_kernel(conv_ref, ret_ref, gc_ref, gr_ref, x_ref, wc_ref, wr_ref, wo_ref, lng_ref, lnb_ref,
                  o_ref, os_ref, *, alpha):
    y_conv = jnp.dot(conv_ref[...], wc_ref[...], preferred_element_type=F32)
    y_ret = jnp.dot(ret_ref[...], wr_ref[...], preferred_element_type=F32)
    h = _sigmoid(gc_ref[...].astype(F32)) * y_conv + _sigmoid(gr_ref[...].astype(F32)) * y_ret
    mix = jnp.dot(h.astype(BF16), wo_ref[...], preferred_element_type=F32)
    y = _layernorm(alpha * x_ref[...] + mix, lng_ref[...], lnb_ref[...])
    o_ref[...] = y
    _store_slabs(os_ref, y)


def _merge(conv_o, ret_o, proj, x2d, wc, wr, wo, ln_g, ln_b, *, tm, gate_col0, alpha):
    t, d = x2d.shape
    ch = conv_o.shape[1]
    v_w = ret_o.shape[1]
    gcol = gate_col0 // d
    const = lambda i: (0, 0)
    resident = functools.partial(pl.BlockSpec, index_map=const, pipeline_mode=pl.Buffered(1))
    return pl.pallas_call(
        functools.partial(_merge_kernel, alpha=alpha),
        grid=(t // tm,),
        in_specs=[
            pl.BlockSpec((tm, ch), lambda i: (i, 0)),
            pl.BlockSpec((tm, v_w), lambda i: (i, 0)),
            pl.BlockSpec((tm, d), lambda i: (i, gcol)),
            pl.BlockSpec((tm, d), lambda i: (i, gcol + 1)),
            pl.BlockSpec((tm, d), lambda i: (i, 0)),
            resident(wc.shape),
            resident(wr.shape),
            resident(wo.shape),
            pl.BlockSpec((1, d), const),
            pl.BlockSpec((1, d), const),
        ],
        out_specs=[pl.BlockSpec((tm, d), lambda i: (i, 0)),
                   pl.BlockSpec((tm * (d // LANES), LANES), lambda i: (i, 0))],
        out_shape=[jax.ShapeDtypeStruct((t, d), F32),
                   jax.ShapeDtypeStruct((t * (d // LANES), LANES), F32)],
        compiler_params=_cparams(("parallel",)),
        name="merge",
    )(conv_o, ret_o, proj, proj, x2d, wc, wr, wo, ln_g, ln_b)


def _router_kernel(x_ref, w_ref, b_ref, tri_ref, idx_ref, gate_ref, rank_ref, cnt_ref, carry_ref):
    i = pl.program_id(0)

    @pl.when(i == 0)
    def _():
        carry_ref[...] = jnp.zeros_like(carry_ref)

    tb = x_ref.shape[0]
    n_e = w_ref.shape[1]
    logits = jnp.dot(x_ref[...], w_ref[...], preferred_element_type=F32,
                     precision=lax.Precision.HIGHEST) + b_ref[...]
    lane_e = lax.broadcasted_iota(jnp.int32, (tb, n_e), 1).astype(F32)
    work = logits
    vals, idxs = [], []
    for _ in range(TOP_K):
        m = jnp.max(work, axis=-1, keepdims=True)
        sel = jnp.min(jnp.where(work == m, lane_e, float(n_e)), axis=-1, keepdims=True)
        vals.append(m)
        idxs.append(sel)
        work = jnp.where(lane_e == sel, -jnp.inf, work)
    exps = [jnp.exp(v - vals[0]) for v in vals]
    denom = exps[0]
    for e in exps[1:]:
        denom = denom + e
    multihot = jnp.zeros((tb, n_e), F32)
    for sel in idxs:
        multihot = multihot + (lane_e == sel).astype(F32)
    before = jnp.dot(tri_ref[...], multihot.astype(BF16), preferred_element_type=F32) + carry_ref[...]
    carry_ref[...] = carry_ref[...] + jnp.sum(multihot, axis=0, keepdims=True)
    cnt_ref[...] = carry_ref[...].astype(jnp.int32)

    lane_o = lax.broadcasted_iota(jnp.int32, (tb, LANES), 1)
    idx_o = jnp.zeros((tb, LANES), jnp.int32)
    gate_o = jnp.zeros((tb, LANES), F32)
    rank_o = jnp.zeros((tb, LANES), jnp.int32)
    for k in range(TOP_K):
        rk = jnp.sum(jnp.where(lane_e == idxs[k], before, 0.0), axis=-1, keepdims=True).astype(jnp.int32)
        idx_o = jnp.where(lane_o == k, idxs[k].astype(jnp.int32), idx_o)
        gate_o = jnp.where(lane_o == k, exps[k] / denom, gate_o)
        rank_o = jnp.where(lane_o == k, rk, rank_o)
    idx_ref[...] = idx_o
    gate_ref[...] = gate_o
    rank_ref[...] = rank_o


def _router(x1, w_router, b_router, *, tb):
    t, d = x1.shape
    n_e = w_router.shape[1]
    tri = (jnp.arange(tb)[:, None] > jnp.arange(tb)[None, :]).astype(BF16)
    const = lambda i: (0, 0)
    row = lambda i: (i, 0)
    return pl.pallas_call(
        _router_kernel,
        grid=(t // tb,),
        in_specs=[
            pl.BlockSpec((tb, d), row),
            pl.BlockSpec((d, n_e), const),
            pl.BlockSpec((1, n_e), const),
            pl.BlockSpec((tb, tb), const),
        ],
        out_specs=[
            pl.BlockSpec((tb, LANES), row),
            pl.BlockSpec((tb, LANES), row),
            pl.BlockSpec((tb, LANES), row),
            pl.BlockSpec((1, n_e), const),
        ],
        out_shape=[
            jax.ShapeDtypeStruct((t, LANES), jnp.int32),
            jax.ShapeDtypeStruct((t, LANES), F32),
            jax.ShapeDtypeStruct((t, LANES), jnp.int32),
            jax.ShapeDtypeStruct((1, n_e), jnp.int32),
        ],
        scratch_shapes=[pltpu.VMEM((1, n_e), F32)],
        compiler_params=_cparams(("arbitrary",)),
        name="router",
    )(x1, w_router, b_router, tri)


def _dispatch_kernel(zstart_ref, zflag_ref, tail_ref, pos_hbm, x_hbm, xs_hbm, pos_smem, zero_ref, sem_idx,
                     sem_row, *, tb, tm, zrows, sp):
    i = pl.program_id(0)
    n_e = zstart_ref.shape[0]

    def slab(ref, tok, n_tok):
        return ref.at[pl.ds(pl.multiple_of(tok * sp, sp), n_tok * sp)]

    @pl.when(i == 0)
    def _():
        zero_ref[...] = jnp.zeros_like(zero_ref)
        pieces = tm // zrows

        def zcopy(e, p):
            return pltpu.make_async_copy(zero_ref, slab(xs_hbm, zstart_ref[e] + p * zrows, zrows), sem_row)

        def start(e, c):
            @pl.when(zflag_ref[e] > 0)
            def _():
                for p in range(pieces):
                    zcopy(e, p).start()
            return c

        def wait(e, c):
            @pl.when(zflag_ref[e] > 0)
            def _():
                for p in range(pieces):
                    zcopy(e, p).wait()
            return c

        lax.fori_loop(0, n_e, start, 0)
        lax.fori_loop(0, n_e, wait, 0)

        def tail(blk, c):
            for p in range(pieces):
                cp = pltpu.make_async_copy(zero_ref, slab(xs_hbm, blk * tm + p * zrows, zrows), sem_row)
                cp.start()
                cp.wait()
            return c

        lax.fori_loop(tail_ref[0], xs_hbm.shape[0] // (sp * tm), tail, 0)

    idx_copy = pltpu.make_async_copy(pos_hbm.at[pl.ds(i * (tb * TOP_K), tb * TOP_K)], pos_smem, sem_idx)
    idx_copy.start()
    idx_copy.wait()

    def row_copy(tok, p):
        return pltpu.make_async_copy(slab(x_hbm, tok, 1), slab(xs_hbm, p, 1), sem_row)

    def issue(r, c):
        for k in range(TOP_K):
            row_copy(i * tb + r, pos_smem[r * TOP_K + k]).start()
        return c

    lax.fori_loop(0, tb, issue, 0)
    pltpu.make_async_copy(slab(x_hbm, 0, tb * TOP_K), slab(xs_hbm, 0, tb * TOP_K), sem_row).wait()


def _dispatch(x1s, pos_flat, zstart, zflag, tail_blk, *, n_tok, n_pad, tb, tm, zrows):
    sp = x1s.shape[0] // n_tok
    return pl.pallas_call(
        functools.partial(_dispatch_kernel, tb=tb, tm=tm, zrows=zrows, sp=sp),
        grid_spec=pltpu.PrefetchScalarGridSpec(
            num_scalar_prefetch=3,
            grid=(n_tok // tb,),
            in_specs=[pl.BlockSpec(memory_space=pl.ANY), pl.BlockSpec(memory_space=pl.ANY)],
            out_specs=pl.BlockSpec(memory_space=pl.ANY),
            scratch_shapes=[
                pltpu.SMEM((tb * TOP_K,), jnp.int32),
                pltpu.VMEM((zrows * sp, LANES), x1s.dtype),
                pltpu.SemaphoreType.DMA,
                pltpu.SemaphoreType.DMA,
            ],
        ),
        out_shape=jax.ShapeDtypeStruct((n_pad * sp, LANES), x1s.dtype),
        compiler_params=_cparams(("arbitrary",)),
        name="dispatch",
    )(zstart, zflag, tail_blk, pos_flat, x1s)


def _moe_kernel(be_ref, nv_ref, last_ref, x_ref, wg_ref, wu_ref, wd_ref, bg_ref, bu_ref, bd_ref, o_ref,
                xb_ref, acc_ref):
    b = pl.program_id(0)
    f = pl.program_id(1)
    n_f = pl.num_programs(1)

    @pl.when(nv_ref[b] > 0)
    def _():
        tm, d = xb_ref.shape
        s_per = d // LANES

        @pl.when(f == 0)
        def _():
            for s in range(s_per):
                xb_ref[:, s * LANES:(s + 1) * LANES] = _load_slab_cols(x_ref, s, tm, s_per).astype(BF16)

        xb = xb_ref[...]
        gate = jnp.dot(xb, wg_ref[...].astype(BF16), preferred_element_type=F32) + bg_ref[...]
        up = jnp.dot(xb, wu_ref[...].astype(BF16), preferred_element_type=F32) + bu_ref[...]
        gate = jnp.minimum(gate, SWIGLU_LIMIT)
        up = jnp.clip(up, -SWIGLU_LIMIT, SWIGLU_LIMIT)
        hdn = (up + 1.0) * (gate * _sigmoid(SWIGLU_ALPHA * gate))
        part = jnp.dot(hdn.astype(BF16), wd_ref[...].astype(BF16), preferred_element_type=F32)

        @pl.when(f == 0)
        def _():
            acc_ref[...] = part

        @pl.when(f > 0)
        def _():
            acc_ref[...] += part

        @pl.when(f == n_f - 1)
        def _():
            _store_slabs(o_ref, acc_ref[...] + bd_ref[...])

    @pl.when(jnp.logical_and(nv_ref[b] == 0, f == n_f - 1))
    def _():
        o_ref[...] = jnp.zeros_like(o_ref)


def _moe_ffn(xs, block_e, nvalid, last_blk, w_gate_up, b_gate_up, w_down, b_down, *, tm, tf):
    n_e, d, two_ff = w_gate_up.shape
    s_per = d // LANES
    n_pad = xs.shape[0] // s_per
    d_ff = two_ff // 2
    n_f = d_ff // tf
    n_blocks = n_pad // tm

    def blk(b, last):
        return jnp.minimum(b, last[0])

    def ftile(b, f, nv):
        return jnp.where(nv[b] > 0, f, n_f - 1)

    return pl.pallas_call(
        _moe_kernel,
        grid_spec=pltpu.PrefetchScalarGridSpec(
            num_scalar_prefetch=3,
            grid=(n_blocks, n_f),
            in_specs=[
                pl.BlockSpec((tm * s_per, LANES), lambda b, f, be, nv, last: (blk(b, last), 0)),
                pl.BlockSpec((None, d, tf), lambda b, f, be, nv, last: (be[b], 0, ftile(b, f, nv))),
                pl.BlockSpec((None, d, tf), lambda b, f, be, nv, last: (be[b], 0, n_f + ftile(b, f, nv))),
                pl.BlockSpec((None, tf, d), lambda b, f, be, nv, last: (be[b], ftile(b, f, nv), 0)),
                pl.BlockSpec((None, 1, tf), lambda b, f, be, nv, last: (be[b], 0, ftile(b, f, nv))),
                pl.BlockSpec((None, 1, tf), lambda b, f, be, nv, last: (be[b], 0, n_f + ftile(b, f, nv))),
                pl.BlockSpec((None, 1, d), lambda b, f, be, nv, last: (be[b], 0, 0)),
            ],
            out_specs=pl.BlockSpec((tm * s_per, LANES), lambda b, f, be, nv, last: (b, 0)),
            scratch_shapes=[pltpu.VMEM((tm, d), BF16), pltpu.VMEM((tm, d), F32)],
        ),
        out_shape=jax.ShapeDtypeStruct((n_pad * s_per, LANES), F32),
        compiler_params=_cparams(("arbitrary", "arbitrary")),
        name="moe_ffn",
    )(block_e, nvalid, last_blk, xs, w_gate_up, w_gate_up, w_down,
      b_gate_up.reshape(n_e, 1, two_ff), b_gate_up.reshape(n_e, 1, two_ff), b_down.reshape(n_e, 1, d))


def _combine_kernel(pos_hbm, yb_hbm, x_ref, gate_ref, lng_ref, lnb_ref, o_ref, pos_smem, buf_ref, ffn_ref,
                    sem_idx, sem_row, *, tb, alpha):
    i = pl.program_id(0)
    s_per = x_ref.shape[1] // LANES
    idx_copy = pltpu.make_async_copy(pos_hbm.at[pl.ds(i * (tb * TOP_K), tb * TOP_K)], pos_smem, sem_idx)
    idx_copy.start()
    idx_copy.wait()

    def issue(r, c):
        for k in range(TOP_K):
            src = pl.multiple_of(pos_smem[r * TOP_K + k] * s_per, s_per)
            dst = pl.multiple_of(r * s_per, s_per)
            pltpu.make_async_copy(yb_hbm.at[pl.ds(src, s_per)], buf_ref.at[k, pl.ds(dst, s_per)],
                                  sem_row).start()
        return c

    lax.fori_loop(0, tb, issue, 0)
    for k in range(TOP_K):
        pltpu.make_async_copy(yb_hbm.at[pl.ds(0, tb * s_per)], buf_ref.at[k], sem_row).wait()

    for s in range(s_per):
        piece = gate_ref[:, 0:1] * _load_slab_cols(buf_ref.at[0], s, tb, s_per)
        for k in range(1, TOP_K):
            piece = piece + gate_ref[:, k:k + 1] * _load_slab_cols(buf_ref.at[k], s, tb, s_per)
        ffn_ref[:, s * LANES:(s + 1) * LANES] = piece
    o_ref[...] = _layernorm(alpha * x_ref[...] + ffn_ref[...], lng_ref[...], lnb_ref[...])


def _combine(pos_flat, yb, x1, gates, ln_g, ln_b, *, tb, alpha):
    t, d = x1.shape
    s_per = d // LANES
    const = lambda i: (0, 0)
    row = lambda i: (i, 0)
    return pl.pallas_call(
        functools.partial(_combine_kernel, tb=tb, alpha=alpha),
        grid=(t // tb,),
        in_specs=[
            pl.BlockSpec(memory_space=pl.ANY),
            pl.BlockSpec(memory_space=pl.ANY),
            pl.BlockSpec((tb, d), row),
            pl.BlockSpec((tb, LANES), row),
            pl.BlockSpec((1, d), const),
            pl.BlockSpec((1, d), const),
        ],
        out_specs=pl.BlockSpec((tb, d), row),
        out_shape=jax.ShapeDtypeStruct((t, d), F32),
        scratch_shapes=[
            pltpu.SMEM((tb * TOP_K,), jnp.int32),
            pltpu.VMEM((TOP_K, tb * s_per, LANES), yb.dtype),
            pltpu.VMEM((tb, d), F32),
            pltpu.SemaphoreType.DMA,
            pltpu.SemaphoreType.DMA,
        ],
        compiler_params=_cparams(("arbitrary",)),
        name="combine",
    )(pos_flat, yb, x1, gates, ln_g, ln_b)


def _tiles(batch, seq):
    t = batch * seq
    return dict(
        proj_tm=min(1024, seq), proj_tn=512,
        conv_tc=min(256, seq),
        merge_tm=min(256, t),
        router_tb=min(512, t),
        dispatch_tb=min(512, t // TOP_K),
        moe_tm=min(512, max(128, t // 8)), moe_tf=512,
        combine_tb=min(256, t),
    )


def _layer(x2d, p, *, batch, seq, depth, cfg):
    t, d = x2d.shape
    alpha = (2 * depth) ** 0.25
    q_w = RET_HEADS * RET_DK
    v_w = RET_HEADS * RET_DV
    ch = p["w_dw"].shape[1]
    n_e = p["w_router"].shape[1]
    row2 = lambda a: a.reshape(1, -1)

    cos_t, sin_t = _rope_tables(seq)
    proj = _in_proj(x2d, p["w_in"].astype(BF16), cos_t, sin_t, seq=seq,
                    tm=cfg["proj_tm"], tn=cfg["proj_tn"], rope_cols=2 * q_w)
    ret_o = _retention(proj, _retention_tables(), row2(p["ret_gn_g"]), row2(p["ret_gn_b"]),
                       batch=batch, seq=seq)
    conv_o = _conv_branch(proj, p["w_dw"], row2(p["b_dw"]), row2(p["conv_ln_g"]), row2(p["conv_ln_b"]),
                          batch=batch, seq=seq, tc=cfg["conv_tc"], col0=2 * q_w + 2 * v_w)
    x1, x1s = _merge(conv_o, ret_o, proj, x2d, p["w_conv_out"].astype(BF16), p["w_ret_out"].astype(BF16),
                     p["w_o"].astype(BF16), row2(p["ln1_g"]), row2(p["ln1_b"]),
                     tm=cfg["merge_tm"], gate_col0=2 * q_w + 2 * v_w + 2 * ch, alpha=alpha)

    idx_o, gate_o, rank_o, counts = _router(x1, p["w_router"], row2(p["b_router"]), tb=cfg["router_tb"])

    tm = cfg["moe_tm"]
    counts = counts[0]
    padded = ((counts + tm - 1) // tm) * tm
    pad_end = jnp.cumsum(padded)
    pad_start = pad_end - padded
    pos = pad_start[idx_o[:, :TOP_K]] + rank_o[:, :TOP_K]
    pos_flat = pos.reshape(-1).astype(jnp.int32)
    n_pad = t * TOP_K + n_e * tm
    n_blocks = n_pad // tm
    blk_row0 = jnp.arange(n_blocks, dtype=jnp.int32) * tm
    block_e = jnp.minimum(jnp.searchsorted(pad_end, blk_row0, side="right"), n_e - 1).astype(jnp.int32)
    nvalid = jnp.clip(counts[block_e] - (blk_row0 - pad_start[block_e]), 0, tm).astype(jnp.int32)
    last_blk = (pad_end[-1:] // tm - 1).astype(jnp.int32)
    zstart = (pad_end - tm).astype(jnp.int32)
    zflag = (padded > 0).astype(jnp.int32)

    xs = _dispatch(x1s, pos_flat, zstart, zflag, last_blk + 1, n_tok=t, n_pad=n_pad, tb=cfg["dispatch_tb"],
                   tm=tm, zrows=min(256, tm))
    yb = _moe_ffn(xs, block_e, nvalid, last_blk, p["w_gate_up"], p["b_gate_up"], p["w_down"], p["b_down"],
                  tm=tm, tf=cfg["moe_tf"])
    return _combine(pos_flat, yb, x1, gate_o, row2(p["ln2_g"]), row2(p["ln2_b"]),
                    tb=cfg["combine_tb"], alpha=alpha)


_PARAM_NAMES = ("w_in", "w_dw", "b_dw", "conv_ln_g", "conv_ln_b", "w_conv_out", "ret_gn_g", "ret_gn_b",
                "w_ret_out", "w_o", "ln1_g", "ln1_b", "w_router", "b_router", "w_gate_up", "b_gate_up",
                "w_down", "b_down", "ln2_g", "ln2_b")


def _forward(x, params, cfg=None):
    batch, seq, d = x.shape
    depth = params[0].shape[0]
    cfg = cfg or _tiles(batch, seq)
    x2d = x.reshape(batch * seq, d)
    for l in range(depth):
        p = {name: w[l] for name, w in zip(_PARAM_NAMES, params)}
        x2d = _layer(x2d, p, batch=batch, seq=seq, depth=depth, cfg=cfg)
    return x2d.reshape(batch, seq, d)


def kernel(x, w_in, w_dw, b_dw, conv_ln_g, conv_ln_b, w_conv_out, ret_gn_g, ret_gn_b, w_ret_out, w_o,
           ln1_g, ln1_b, w_router, b_router, w_gate_up, b_gate_up, w_down, b_down, ln2_g, ln2_b):
    return _forward(x, (w_in, w_dw, b_dw, conv_ln_g, conv_ln_b, w_conv_out, ret_gn_g, ret_gn_b, w_ret_out,
                        w_o, ln1_g, ln1_b, w_router, b_router, w_gate_up, b_gate_up, w_down, b_down,
                        ln2_g, ln2_b))
```

```python
import functools

import jax
import jax.numpy as jnp
import numpy as np
from jax import lax
from jax.experimental import pallas as pl
from jax.experimental.pallas import tpu as pltpu

RET_HEADS = 8
RET_DK = 128
RET_DV = 256
RET_CHUNK = 128
ROPE_BASE = 10000.0
CONV_K = 31
TOP_K = 4
SWIGLU_LIMIT = 7.0
SWIGLU_ALPHA = 1.702
LN_EPS = 1e-5

LANES = 128
CONV_HALO = 32
VMEM_LIMIT = 56 * 1024 * 1024

F32 = jnp.float32
BF16 = jnp.bfloat16


def _cparams(sem):
    return pltpu.CompilerParams(dimension_semantics=sem, vmem_limit_bytes=VMEM_LIMIT)


def _layernorm(x, g, b):
    mu = jnp.mean(x, axis=-1, keepdims=True)
    xc = x - mu
    var = jnp.mean(xc * xc, axis=-1, keepdims=True)
    return xc * lax.rsqrt(var + LN_EPS) * g + b


def _sigmoid(x):
    return 1.0 / (1.0 + jnp.exp(-x))


def _store_slabs(slab_ref, y):
    rows, d = y.shape
    s_per = d // LANES
    for s in range(s_per):
        slab_ref[pl.ds(s, rows, stride=s_per), :] = y[:, s * LANES:(s + 1) * LANES].astype(slab_ref.dtype)


def _load_slab_cols(slab_ref, s, rows, s_per):
    return slab_ref[pl.ds(s, rows, stride=s_per), :]


def _in_proj_kernel(x_ref, w_ref, cos_ref, sin_ref, o_ref, xb_ref, *, n_rope_tiles):
    j = pl.program_id(1)

    @pl.when(j == 0)
    def _():
        xb_ref[...] = x_ref[...].astype(BF16)

    acc = jnp.dot(xb_ref[...], w_ref[...], preferred_element_type=F32)

    @pl.when(j < n_rope_tiles)
    def _():
        cos = cos_ref[...]
        sin = sin_ref[...]
        for g in range(acc.shape[1] // LANES):
            a = acc[:, g * LANES:(g + 1) * LANES]
            r = a * cos + pltpu.roll(a, LANES // 2, 1) * sin
            o_ref[:, g * LANES:(g + 1) * LANES] = r.astype(o_ref.dtype)

    @pl.when(j >= n_rope_tiles)
    def _():
        o_ref[...] = acc.astype(o_ref.dtype)


def _in_proj(x2d, w_bf16, cos_t, sin_t, *, seq, tm, tn, rope_cols):
    t, d = x2d.shape
    n = w_bf16.shape[1]
    blocks_per_seq = seq // tm
    return pl.pallas_call(
        functools.partial(_in_proj_kernel, n_rope_tiles=rope_cols // tn),
        grid=(t // tm, n // tn),
        in_specs=[
            pl.BlockSpec((tm, d), lambda i, j: (i, 0)),
            pl.BlockSpec((d, tn), lambda i, j: (0, j)),
            pl.BlockSpec((tm, LANES), lambda i, j: (i % blocks_per_seq, 0)),
            pl.BlockSpec((tm, LANES), lambda i, j: (i % blocks_per_seq, 0)),
        ],
        out_specs=pl.BlockSpec((tm, tn), lambda i, j: (i, j)),
        out_shape=jax.ShapeDtypeStruct((t, n), BF16),
        scratch_shapes=[pltpu.VMEM((tm, d), BF16)],
        compiler_params=_cparams(("parallel", "arbitrary")),
        name="in_proj",
    )(x2d, w_bf16, cos_t, sin_t)


def _retention_kernel(q_ref, k_ref, v_ref, g_ref, mask_ref, qdec_ref, kdec_ref, cdec_ref,
                      gng_ref, gnb_ref, o_ref, state_ref):
    n = pl.program_id(1)

    @pl.when(n == 0)
    def _():
        state_ref[...] = jnp.zeros_like(state_ref)

    for h in range(RET_HEADS):
        ks = slice(h * RET_DK, (h + 1) * RET_DK)
        vs = slice(h * RET_DV, (h + 1) * RET_DV)
        qh = q_ref[:, ks]
        kh = k_ref[:, ks]
        vh = v_ref[:, vs]
        s = lax.dot_general(qh, kh, (((1,), (1,)), ((), ())), preferred_element_type=F32)
        s = s * mask_ref[h]
        intra = jnp.dot(s.astype(BF16), vh, preferred_element_type=F32)
        st = state_ref[h]
        cross = jnp.dot(qh, st.astype(BF16), preferred_element_type=F32) * qdec_ref[h]
        vd = (vh.astype(F32) * kdec_ref[h]).astype(BF16)
        kv = jnp.dot(kh.T, vd, preferred_element_type=F32)
        state_ref[h] = cdec_ref[h] * st + kv
        y = _layernorm(intra + cross, gng_ref[:, vs], gnb_ref[:, vs])
        gg = g_ref[:, vs].astype(F32)
        o_ref[:, vs] = (gg * _sigmoid(gg) * y).astype(o_ref.dtype)


def _retention(proj, tables, gn_g, gn_b, *, batch, seq):
    t = proj.shape[0]
    c = RET_CHUNK
    n_chunks = seq // c
    q_w = RET_HEADS * RET_DK
    v_w = RET_HEADS * RET_DV
    mask, qdec, kdec, cdec = tables
    row = lambda b, n: b * n_chunks + n
    const3 = lambda b, n: (0, 0, 0)
    const2 = lambda b, n: (0, 0)
    return pl.pallas_call(
        _retention_kernel,
        grid=(batch, n_chunks),
        in_specs=[
            pl.BlockSpec((c, q_w), lambda b, n: (row(b, n), 0)),
            pl.BlockSpec((c, q_w), lambda b, n: (row(b, n), 1)),
            pl.BlockSpec((c, v_w), lambda b, n: (row(b, n), 2 * q_w // v_w)),
            pl.BlockSpec((c, v_w), lambda b, n: (row(b, n), 2 * q_w // v_w + 1)),
            pl.BlockSpec(mask.shape, const3),
            pl.BlockSpec(qdec.shape, const3),
            pl.BlockSpec(kdec.shape, const3),
            pl.BlockSpec(cdec.shape, const3),
            pl.BlockSpec((1, v_w), const2),
            pl.BlockSpec((1, v_w), const2),
        ],
        out_specs=pl.BlockSpec((c, v_w), lambda b, n: (row(b, n), 0)),
        out_shape=jax.ShapeDtypeStruct((t, v_w), BF16),
        scratch_shapes=[pltpu.VMEM((RET_HEADS, RET_DK, RET_DV), F32)],
        compiler_params=_cparams(("parallel", "arbitrary")),
        name="retention",
    )(proj, proj, proj, proj, mask, qdec, kdec, cdec, gn_g, gn_b)


def _retention_tables():
    log_gamma = jnp.log(1.0 - jnp.exp2(-5.0 - jnp.arange(RET_HEADS, dtype=F32)))
    idx = jnp.arange(RET_CHUNK, dtype=F32)
    diff = idx[:, None] - idx[None, :]
    scale = RET_DK ** -0.5
    mask = jnp.where(diff[None] >= 0,
                     jnp.exp(jnp.maximum(diff, 0.0)[None] * log_gamma[:, None, None]), 0.0) * scale
    k_decay = jnp.exp((RET_CHUNK - 1 - idx)[None, :] * log_gamma[:, None]) * scale
    q_decay = jnp.exp((idx + 1.0)[None, :] * log_gamma[:, None])
    chunk_decay = jnp.exp(RET_CHUNK * log_gamma)
    bshape = (RET_HEADS, RET_CHUNK, RET_DV)
    qdec = jnp.broadcast_to(q_decay[:, :, None], bshape)
    kdec = jnp.broadcast_to(k_decay[:, :, None], bshape)
    cdec = jnp.broadcast_to(chunk_decay[:, None, None], (RET_HEADS, 1, RET_DV))
    return mask.astype(F32), qdec.astype(F32), kdec.astype(F32), cdec.astype(F32)


def _rope_tables(seq):
    half = RET_DK // 2
    inv_freq = ROPE_BASE ** (-jnp.arange(half, dtype=F32) / half)
    ang = jnp.arange(seq, dtype=F32)[:, None] * inv_freq[None, :]
    cos, sin = jnp.cos(ang), jnp.sin(ang)
    return jnp.concatenate([cos, cos], axis=-1), jnp.concatenate([-sin, sin], axis=-1)


def _conv_kernel(cv_ref, cg_ref, pv_ref, pg_ref, w_ref, b_ref, lng_ref, lnb_ref, o_ref, u_ref, *, rows):
    i = pl.program_id(1)
    tc = cv_ref.shape[0]
    cv = cv_ref[...].astype(F32)
    cg = cg_ref[...].astype(F32)
    pv = pv_ref[...].astype(F32)
    pg = pg_ref[...].astype(F32)
    prev = pv * _sigmoid(pg)
    u_ref[0:CONV_HALO, :] = jnp.where(i == 0, 0.0, prev)
    u_ref[CONV_HALO:CONV_HALO + tc, :] = cv * _sigmoid(cg)
    off = CONV_HALO - (CONV_K - 1)
    for r in range(tc // rows):
        base = r * rows
        acc = jnp.zeros((rows, cv_ref.shape[1]), F32)
        for j in range(CONV_K):
            acc = acc + w_ref[j:j + 1, :] * u_ref[base + off + j:base + off + j + rows, :]
        y = _layernorm(acc + b_ref[...], lng_ref[...], lnb_ref[...])
        o_ref[base:base + rows, :] = (y * _sigmoid(y)).astype(o_ref.dtype)


def _conv_branch(proj, w_dw, b_dw, ln_g, ln_b, *, batch, seq, tc, col0, rows=16):
    t = proj.shape[0]
    ch = w_dw.shape[1]
    nblk = seq // tc
    cv_col = col0 // ch
    halo_per_blk = tc // CONV_HALO
    cur = lambda b, i: b * nblk + i
    prev = lambda b, i: jnp.maximum((b * nblk + i) * halo_per_blk - 1, 0)
    const = lambda b, i: (0, 0)
    return pl.pallas_call(
        functools.partial(_conv_kernel, rows=rows),
        grid=(batch, nblk),
        in_specs=[
            pl.BlockSpec((tc, ch), lambda b, i: (cur(b, i), cv_col)),
            pl.BlockSpec((tc, ch), lambda b, i: (cur(b, i), cv_col + 1)),
            pl.BlockSpec((CONV_HALO, ch), lambda b, i: (prev(b, i), cv_col)),
            pl.BlockSpec((CONV_HALO, ch), lambda b, i: (prev(b, i), cv_col + 1)),
            pl.BlockSpec((CONV_K, ch), const),
            pl.BlockSpec((1, ch), const),
            pl.BlockSpec((1, ch), const),
            pl.BlockSpec((1, ch), const),
        ],
        out_specs=pl.BlockSpec((tc, ch), lambda b, i: (cur(b, i), 0)),
        out_shape=jax.ShapeDtypeStruct((t, ch), BF16),
        scratch_shapes=[pltpu.VMEM((CONV_HALO + tc, ch), F32)],
        compiler_params=_cparams(("parallel", "arbitrary")),
        name="conv_branch",
    )(proj, proj, proj, proj, w_dw, b_dw, ln_g, ln_b)


def _merge_kernel(conv_ref, ret_ref, gc_ref, gr_ref, x_ref, wc_ref, wr_ref, wo_ref, lng_ref, lnb_ref,
                  o_ref, os_ref, *, alpha):
    y_conv = jnp.dot(conv_ref[...], wc_ref[...], preferred_element_type=F32)
    y_ret = jnp.dot(ret_ref[...], wr_ref[...], preferred_element_type=F32)
    h = _sigmoid(gc_ref[...].astype(F32)) * y_conv + _sigmoid(gr_ref[...].astype(F32)) * y_ret
    mix = jnp.dot(h.astype(BF16), wo_ref[...], preferred_element_type=F32)
    y = _layernorm(alpha * x_ref[...] + mix, lng_ref[...], lnb_ref[...])
    o_ref[...] = y
    _store_slabs(os_ref, y)


def _merge(conv_o, ret_o, proj, x2d, wc, wr, wo, ln_g, ln_b, *, tm, gate_col0, alpha):
    t, d = x2d.shape
    ch = conv_o.shape[1]
    v_w = ret_o.shape[1]
    gcol = gate_col0 // d
    const = lambda i: (0, 0)
    resident = functools.partial(pl.BlockSpec, index_map=const, pipeline_mode=pl.Buffered(1))
    return pl.pallas_call(
        functools.partial(_merge_kernel, alpha=alpha),
        grid=(t // tm,),
        in_specs=[
            pl.BlockSpec((tm, ch), lambda i: (i, 0)),
            pl.BlockSpec((tm, v_w), lambda i: (i, 0)),
            pl.BlockSpec((tm, d), lambda i: (i, gcol)),
            pl.BlockSpec((tm, d), lambda i: (i, gcol + 1)),
            pl.BlockSpec((tm, d), lambda i: (i, 0)),
            resident(wc.shape),
            resident(wr.shape),
            resident(wo.shape),
            pl.BlockSpec((1, d), const),
            pl.BlockSpec((1, d), const),
        ],
        out_specs=[pl.BlockSpec((tm, d), lambda i: (i, 0)),
                   pl.BlockSpec((tm * (d // LANES), LANES), lambda i: (i, 0))],
        out_shape=[jax.ShapeDtypeStruct((t, d), F32),
                   jax.ShapeDtypeStruct((t * (d // LANES), LANES), F32)],
        compiler_params=_cparams(("parallel",)),
        name="merge",
    )(conv_o, ret_o, proj, proj, x2d, wc, wr, wo, ln_g, ln_b)


def _router_kernel(x_ref, w_ref, b_ref, tri_ref, idx_ref, gate_ref, rank_ref, cnt_ref, carry_ref):
    i = pl.program_id(0)

    @pl.when(i == 0)
    def _():
        carry_ref[...] = jnp.zeros_like(carry_ref)

    tb = x_ref.shape[0]
    n_e = w_ref.shape[1]
    logits = jnp.dot(x_ref[...], w_ref[...], preferred_element_type=F32,
                     precision=lax.Precision.HIGHEST) + b_ref[...]
    lane_e = lax.broadcasted_iota(jnp.int32, (tb, n_e), 1).astype(F32)
    work = logits
    vals, idxs = [], []
    for _ in range(TOP_K):
        m = jnp.max(work, axis=-1, keepdims=True)
        sel = jnp.min(jnp.where(work == m, lane_e, float(n_e)), axis=-1, keepdims=True)
        vals.append(m)
        idxs.append(sel)
        work = jnp.where(lane_e == sel, -jnp.inf, work)
    exps = [jnp.exp(v - vals[0]) for v in vals]
    denom = exps[0]
    for e in exps[1:]:
        denom = denom + e
    multihot = jnp.zeros((tb, n_e), F32)
    for sel in idxs:
        multihot = multihot + (lane_e == sel).astype(F32)
    before = jnp.dot(tri_ref[...], multihot.astype(BF16), preferred_element_type=F32) + carry_ref[...]
    carry_ref[...] = carry_ref[...] + jnp.sum(multihot, axis=0, keepdims=True)
    cnt_ref[...] = carry_ref[...].astype(jnp.int32)

    lane_o = lax.broadcasted_iota(jnp.int32, (tb, LANES), 1)
    idx_o = jnp.zeros((tb, LANES), jnp.int32)
    gate_o = jnp.zeros((tb, LANES), F32)
    rank_o = jnp.zeros((tb, LANES), jnp.int32)
    for k in range(TOP_K):
        rk = jnp.sum(jnp.where(lane_e == idxs[k], before, 0.0), axis=-1, keepdims=True).astype(jnp.int32)
        idx_o = jnp.where(lane_o == k, idxs[k].astype(jnp.int32), idx_o)
        gate_o = jnp.where(lane_o == k, exps[k] / denom, gate_o)
        rank_o = jnp.where(lane_o == k, rk, rank_o)
    idx_ref[...] = idx_o
    gate_ref[...] = gate_o
    rank_ref[...] = rank_o


def _router(x1, w_router, b_router, *, tb):
    t, d = x1.shape
    n_e = w_router.shape[1]
    tri = (jnp.arange(tb)[:, None] > jnp.arange(tb)[None, :]).astype(BF16)
    const = lambda i: (0, 0)
    row = lambda i: (i, 0)
    return pl.pallas_call(
        _router_kernel,
        grid=(t // tb,),
        in_specs=[
            pl.BlockSpec((tb, d), row),
            pl.BlockSpec((d, n_e), const),
            pl.BlockSpec((1, n_e), const),
            pl.BlockSpec((tb, tb), const),
        ],
        out_specs=[
            pl.BlockSpec((tb, LANES), row),
            pl.BlockSpec((tb, LANES), row),
            pl.BlockSpec((tb, LANES), row),
            pl.BlockSpec((1, n_e), const),
        ],
        out_shape=[
            jax.ShapeDtypeStruct((t, LANES), jnp.int32),
            jax.ShapeDtypeStruct((t, LANES), F32),
            jax.ShapeDtypeStruct((t, LANES), jnp.int32),
            jax.ShapeDtypeStruct((1, n_e), jnp.int32),
        ],
        scratch_shapes=[pltpu.VMEM((1, n_e), F32)],
        compiler_params=_cparams(("arbitrary",)),
        name="router",
    )(x1, w_router, b_router, tri)


def _dispatch_kernel(zstart_ref, zflag_ref, tail_ref, pos_hbm, x_ref, xs_hbm, pos_smem, zero_ref, sem_idx,
                     sem_row, *, tb, tm, zrows, sp):
    i = pl.program_id(0)
    n_e = zstart_ref.shape[0]

    def slab(ref, tok, n_tok):
        return ref.at[pl.ds(pl.multiple_of(tok * sp, sp), n_tok * sp)]

    @pl.when(i == 0)
    def _():
        zero_ref[...] = jnp.zeros_like(zero_ref)
        pieces = tm // zrows

        def zcopy(e, p):
            return pltpu.make_async_copy(zero_ref, slab(xs_hbm, zstart_ref[e] + p * zrows, zrows), sem_row)

        def start(e, c):
            @pl.when(zflag_ref[e] > 0)
            def _():
                for p in range(pieces):
                    zcopy(e, p).start()
            return c

        def wait(e, c):
            @pl.when(zflag_ref[e] > 0)
            def _():
                for p in range(pieces):
                    zcopy(e, p).wait()
            return c

        lax.fori_loop(0, n_e, start, 0)
        lax.fori_loop(0, n_e, wait, 0)

        def tail(blk, c):
            for p in range(pieces):
                cp = pltpu.make_async_copy(zero_ref, slab(xs_hbm, blk * tm + p * zrows, zrows), sem_row)
                cp.start()
                cp.wait()
            return c

        lax.fori_loop(tail_ref[0], xs_hbm.shape[0] // (sp * tm), tail, 0)

    idx_copy = pltpu.make_async_copy(pos_hbm.at[pl.ds(i * (tb * TOP_K), tb * TOP_K)], pos_smem, sem_idx)
    idx_copy.start()
    idx_copy.wait()

    def issue(r, c):
        src = x_ref.at[pl.ds(pl.multiple_of(r * sp, sp), sp)]
        for k in range(TOP_K):
            pltpu.make_async_copy(src, slab(xs_hbm, pos_smem[r * TOP_K + k], 1), sem_row).start()
        return c

    lax.fori_loop(0, tb, issue, 0, unroll=4)
    for k in range(TOP_K):
        pltpu.make_async_copy(x_ref, slab(xs_hbm, 0, tb), sem_row).wait()


def _dispatch(x1s, pos_flat, zstart, zflag, tail_blk, *, n_tok, n_pad, tb, tm, zrows):
    sp = x1s.shape[0] // n_tok
    return pl.pallas_call(
        functools.partial(_dispatch_kernel, tb=tb, tm=tm, zrows=zrows, sp=sp),
        grid_spec=pltpu.PrefetchScalarGridSpec(
            num_scalar_prefetch=3,
            grid=(n_tok // tb,),
            in_specs=[pl.BlockSpec(memory_space=pl.ANY),
                      pl.BlockSpec((tb * sp, LANES), lambda i, zs, zf, tl: (i, 0))],
            out_specs=pl.BlockSpec(memory_space=pl.ANY),
            scratch_shapes=[
                pltpu.SMEM((tb * TOP_K,), jnp.int32),
                pltpu.VMEM((zrows * sp, LANES), x1s.dtype),
                pltpu.SemaphoreType.DMA,
                pltpu.SemaphoreType.DMA,
            ],
        ),
        out_shape=jax.ShapeDtypeStruct((n_pad * sp, LANES), x1s.dtype),
        compiler_params=_cparams(("arbitrary",)),
        name="dispatch",
    )(zstart, zflag, tail_blk, pos_flat, x1s)


def _moe_kernel(be_ref, nv_ref, last_ref, x_ref, wg_ref, wu_ref, wd_ref, bg_ref, bu_ref, bd_ref, o_ref,
                xb_ref, acc_ref):
    b = pl.program_id(0)
    f = pl.program_id(1)
    n_f = pl.num_programs(1)

    @pl.when(nv_ref[b] > 0)
    def _():
        tm, d = xb_ref.shape
        s_per = d // LANES

        @pl.when(f == 0)
        def _():
            for s in range(s_per):
                xb_ref[:, s * LANES:(s + 1) * LANES] = _load_slab_cols(x_ref, s, tm, s_per).astype(BF16)

        xb = xb_ref[...]
        gate = jnp.dot(xb, wg_ref[...].astype(BF16), preferred_element_type=F32) + bg_ref[...]
        up = jnp.dot(xb, wu_ref[...].astype(BF16), preferred_element_type=F32) + bu_ref[...]
        gate = jnp.minimum(gate, SWIGLU_LIMIT)
        up = jnp.clip(up, -SWIGLU_LIMIT, SWIGLU_LIMIT)
        hdn = (up + 1.0) * (gate * _sigmoid(SWIGLU_ALPHA * gate))
        part = jnp.dot(hdn.astype(BF16), wd_ref[...].astype(BF16), preferred_element_type=F32)

        @pl.when(f == 0)
        def _():
            acc_ref[...] = part

        @pl.when(f > 0)
        def _():
            acc_ref[...] += part

        @pl.when(f == n_f - 1)
        def _():
            _store_slabs(o_ref, acc_ref[...] + bd_ref[...])

    @pl.when(jnp.logical_and(nv_ref[b] == 0, f == n_f - 1))
    def _():
        o_ref[...] = jnp.zeros_like(o_ref)


def _moe_ffn(xs, block_e, nvalid, last_blk, w_gate_up, b_gate_up, w_down, b_down, *, tm, tf):
    n_e, d, two_ff = w_gate_up.shape
    s_per = d // LANES
    n_pad = xs.shape[0] // s_per
    d_ff = two_ff // 2
    n_f = d_ff // tf
    n_blocks = n_pad // tm

    def blk(b, last):
        return jnp.minimum(b, last[0])

    def ftile(b, f, nv):
        return jnp.where(nv[b] > 0, f, n_f - 1)

    return pl.pallas_call(
        _moe_kernel,
        grid_spec=pltpu.PrefetchScalarGridSpec(
            num_scalar_prefetch=3,
            grid=(n_blocks, n_f),
            in_specs=[
                pl.BlockSpec((tm * s_per, LANES), lambda b, f, be, nv, last: (blk(b, last), 0)),
                pl.BlockSpec((None, d, tf), lambda b, f, be, nv, last: (be[b], 0, ftile(b, f, nv))),
                pl.BlockSpec((None, d, tf), lambda b, f, be, nv, last: (be[b], 0, n_f + ftile(b, f, nv))),
                pl.BlockSpec((None, tf, d), lambda b, f, be, nv, last: (be[b], ftile(b, f, nv), 0)),
                pl.BlockSpec((None, 1, tf), lambda b, f, be, nv, last: (be[b], 0, ftile(b, f, nv))),
                pl.BlockSpec((None, 1, tf), lambda b, f, be, nv, last: (be[b], 0, n_f + ftile(b, f, nv))),
                pl.BlockSpec((None, 1, d), lambda b, f, be, nv, last: (be[b], 0, 0)),
            ],
            out_specs=pl.BlockSpec((tm * s_per, LANES), lambda b, f, be, nv, last: (b, 0)),
            scratch_shapes=[pltpu.VMEM((tm, d), BF16), pltpu.VMEM((tm, d), F32)],
        ),
        out_shape=jax.ShapeDtypeStruct((n_pad * s_per, LANES), F32),
        compiler_params=_cparams(("arbitrary", "arbitrary")),
        name="moe_ffn",
    )(block_e, nvalid, last_blk, xs, w_gate_up, w_gate_up, w_down,
      b_gate_up.reshape(n_e, 1, two_ff), b_gate_up.reshape(n_e, 1, two_ff), b_down.reshape(n_e, 1, d))


def _combine_kernel(pos_hbm, yb_hbm, x_ref, gate_ref, lng_ref, lnb_ref, o_ref, pos_smem, buf_ref, ffn_ref,
                    sem_idx, sem_row, *, tb, alpha):
    i = pl.program_id(0)
    s_per = x_ref.shape[1] // LANES
    idx_copy = pltpu.make_async_copy(pos_hbm.at[pl.ds(i * (tb * TOP_K), tb * TOP_K)], pos_smem, sem_idx)
    idx_copy.start()
    idx_copy.wait()

    def issue(r, c):
        for k in range(TOP_K):
            src = pl.multiple_of(pos_smem[r * TOP_K + k] * s_per, s_per)
            dst = pl.multiple_of(r * s_per, s_per)
            pltpu.make_async_copy(yb_hbm.at[pl.ds(src, s_per)], buf_ref.at[k, pl.ds(dst, s_per)],
                                  sem_row).start()
        return c

    lax.fori_loop(0, tb, issue, 0)
    for k in range(TOP_K):
        pltpu.make_async_copy(yb_hbm.at[pl.ds(0, tb * s_per)], buf_ref.at[k], sem_row).wait()

    for s in range(s_per):
        piece = gate_ref[:, 0:1] * _load_slab_cols(buf_ref.at[0], s, tb, s_per)
        for k in range(1, TOP_K):
            piece = piece + gate_ref[:, k:k + 1] * _load_slab_cols(buf_ref.at[k], s, tb, s_per)
        ffn_ref[:, s * LANES:(s + 1) * LANES] = piece
    o_ref[...] = _layernorm(alpha * x_ref[...] + ffn_ref[...], lng_ref[...], lnb_ref[...])


def _combine(pos_flat, yb, x1, gates, ln_g, ln_b, *, tb, alpha):
    t, d = x1.shape
    s_per = d // LANES
    const = lambda i: (0, 0)
    row = lambda i: (i, 0)
    return pl.pallas_call(
        functools.partial(_combine_kernel, tb=tb, alpha=alpha),
        grid=(t // tb,),
        in_specs=[
            pl.BlockSpec(memory_space=pl.ANY),
            pl.BlockSpec(memory_space=pl.ANY),
            pl.BlockSpec((tb, d), row),
            pl.BlockSpec((tb, LANES), row),
            pl.BlockSpec((1, d), const),
            pl.BlockSpec((1, d), const),
        ],
        out_specs=pl.BlockSpec((tb, d), row),
        out_shape=jax.ShapeDtypeStruct((t, d), F32),
        scratch_shapes=[
            pltpu.SMEM((tb * TOP_K,), jnp.int32),
            pltpu.VMEM((TOP_K, tb * s_per, LANES), yb.dtype),
            pltpu.VMEM((tb, d), F32),
            pltpu.SemaphoreType.DMA,
            pltpu.SemaphoreType.DMA,
        ],
        compiler_params=_cparams(("arbitrary",)),
        name="combine",
    )(pos_flat, yb, x1, gates, ln_g, ln_b)


def _tiles(batch, seq):
    t = batch * seq
    return dict(
        proj_tm=min(1024, seq), proj_tn=512,
        conv_tc=min(256, seq),
        merge_tm=min(256, t),
        router_tb=min(512, t),
        dispatch_tb=min(512, t // TOP_K),
        moe_tm=min(512, max(128, t // 8)), moe_tf=512,
        combine_tb=min(256, t),
    )


def _layer(x2d, p, *, batch, seq, depth, cfg):
    t, d = x2d.shape
    alpha = (2 * depth) ** 0.25
    q_w = RET_HEADS * RET_DK
    v_w = RET_HEADS * RET_DV
    ch = p["w_dw"].shape[1]
    n_e = p["w_router"].shape[1]
    row2 = lambda a: a.reshape(1, -1)

    cos_t, sin_t = _rope_tables(seq)
    proj = _in_proj(x2d, p["w_in"].astype(BF16), cos_t, sin_t, seq=seq,
                    tm=cfg["proj_tm"], tn=cfg["proj_tn"], rope_cols=2 * q_w)
    ret_o = _retention(proj, _retention_tables(), row2(p["ret_gn_g"]), row2(p["ret_gn_b"]),
                       batch=batch, seq=seq)
    conv_o = _conv_branch(proj, p["w_dw"], row2(p["b_dw"]), row2(p["conv_ln_g"]), row2(p["conv_ln_b"]),
                          batch=batch, seq=seq, tc=cfg["conv_tc"], col0=2 * q_w + 2 * v_w)
    x1, x1s = _merge(conv_o, ret_o, proj, x2d, p["w_conv_out"].astype(BF16), p["w_ret_out"].astype(BF16),
                     p["w_o"].astype(BF16), row2(p["ln1_g"]), row2(p["ln1_b"]),
                     tm=cfg["merge_tm"], gate_col0=2 * q_w + 2 * v_w + 2 * ch, alpha=alpha)

    idx_o, gate_o, rank_o, counts = _router(x1, p["w_router"], row2(p["b_router"]), tb=cfg["router_tb"])

    tm = cfg["moe_tm"]
    counts = counts[0]
    padded = ((counts + tm - 1) // tm) * tm
    pad_end = jnp.cumsum(padded)
    pad_start = pad_end - padded
    pos = pad_start[idx_o[:, :TOP_K]] + rank_o[:, :TOP_K]
    pos_flat = pos.reshape(-1).astype(jnp.int32)
    n_pad = t * TOP_K + n_e * tm
    n_blocks = n_pad // tm
    blk_row0 = jnp.arange(n_blocks, dtype=jnp.int32) * tm
    block_e = jnp.minimum(jnp.searchsorted(pad_end, blk_row0, side="right"), n_e - 1).astype(jnp.int32)
    nvalid = jnp.clip(counts[block_e] - (blk_row0 - pad_start[block_e]), 0, tm).astype(jnp.int32)
    last_blk = (pad_end[-1:] // tm - 1).astype(jnp.int32)
    zstart = (pad_end - tm).astype(jnp.int32)
    zflag = (padded > 0).astype(jnp.int32)

    xs = _dispatch(x1s, pos_flat, zstart, zflag, last_blk + 1, n_tok=t, n_pad=n_pad, tb=cfg["dispatch_tb"],
                   tm=tm, zrows=min(256, tm))
    yb = _moe_ffn(xs, block_e, nvalid, last_blk, p["w_gate_up"], p["b_gate_up"], p["w_down"], p["b_down"],
                  tm=tm, tf=cfg["moe_tf"])
    return _combine(pos_flat, yb, x1, gate_o, row2(p["ln2_g"]), row2(p["ln2_b"]),
                    tb=cfg["combine_tb"], alpha=alpha)


_PARAM_NAMES = ("w_in", "w_dw", "b_dw", "conv_ln_g", "conv_ln_b", "w_conv_out", "ret_gn_g", "ret_gn_b",
                "w_ret_out", "w_o", "ln1_g", "ln1_b", "w_router", "b_router", "w_gate_up", "b_gate_up",
                "w_down", "b_down", "ln2_g", "ln2_b")


def _forward(x, params, cfg=None):
    batch, seq, d = x.shape
    depth = params[0].shape[0]
    cfg = cfg or _tiles(batch, seq)
    x2d = x.reshape(batch * seq, d)
    for l in range(depth):
        p = {name: w[l] for name, w in zip(_PARAM_NAMES, params)}
        x2d = _layer(x2d, p, batch=batch, seq=seq, depth=depth, cfg=cfg)
    return x2d.reshape(batch, seq, d)


def kernel(x, w_in, w_dw, b_dw, conv_ln_g, conv_ln_b, w_conv_out, ret_gn_g, ret_gn_b, w_ret_out, w_o,
           ln1_g, ln1_b, w_router, b_router, w_gate_up, b_gate_up, w_down, b_down, ln2_g, ln2_b):
    return _forward(x, (w_in, w_dw, b_dw, conv_ln_g, conv_ln_b, w_conv_out, ret_gn_g, ret_gn_b, w_ret_out,
                        w_o, ln1_g, ln1_b, w_router, b_router, w_gate_up, b_gate_up, w_down, b_down,
                        ln2_g, ln2_b))
```

```python
import functools

import jax
import jax.numpy as jnp
import numpy as np
from jax import lax
from jax.experimental import pallas as pl
from jax.experimental.pallas import tpu as pltpu

RET_HEADS = 8
RET_DK = 128
RET_DV = 256
RET_CHUNK = 128
ROPE_BASE = 10000.0
CONV_K = 31
TOP_K = 4
SWIGLU_LIMIT = 7.0
SWIGLU_ALPHA = 1.702
LN_EPS = 1e-5

LANES = 128
CONV_HALO = 32
VMEM_LIMIT = 56 * 1024 * 1024

F32 = jnp.float32
BF16 = jnp.bfloat16


def _cparams(sem):
    return pltpu.CompilerParams(dimension_semantics=sem, vmem_limit_bytes=VMEM_LIMIT)


def _layernorm(x, g, b):
    mu = jnp.mean(x, axis=-1, keepdims=True)
    xc = x - mu
    var = jnp.mean(xc * xc, axis=-1, keepdims=True)
    return xc * lax.rsqrt(var + LN_EPS) * g + b


def _sigmoid(x):
    return 1.0 / (1.0 + jnp.exp(-x))


def _store_slabs(slab_ref, y):
    rows, d = y.shape
    s_per = d // LANES
    for s in range(s_per):
        slab_ref[pl.ds(s, rows, stride=s_per), :] = y[:, s * LANES:(s + 1) * LANES].astype(slab_ref.dtype)


def _load_slab_cols(slab_ref, s, rows, s_per):
    return slab_ref[pl.ds(s, rows, stride=s_per), :]


def _in_proj_kernel(x_ref, w_ref, cos_ref, sin_ref, o_ref, xb_ref, *, n_rope_tiles):
    j = pl.program_id(1)

    @pl.when(j == 0)
    def _():
        xb_ref[...] = x_ref[...].astype(BF16)

    acc = jnp.dot(xb_ref[...], w_ref[...], preferred_element_type=F32)

    @pl.when(j < n_rope_tiles)
    def _():
        cos = cos_ref[...]
        sin = sin_ref[...]
        for g in range(acc.shape[1] // LANES):
            a = acc[:, g * LANES:(g + 1) * LANES]
            r = a * cos + pltpu.roll(a, LANES // 2, 1) * sin
            o_ref[:, g * LANES:(g + 1) * LANES] = r.astype(o_ref.dtype)

    @pl.when(j >= n_rope_tiles)
    def _():
        o_ref[...] = acc.astype(o_ref.dtype)


def _in_proj(x2d, w_bf16, cos_t, sin_t, *, seq, tm, tn, rope_cols):
    t, d = x2d.shape
    n = w_bf16.shape[1]
    blocks_per_seq = seq // tm
    return pl.pallas_call(
        functools.partial(_in_proj_kernel, n_rope_tiles=rope_cols // tn),
        grid=(t // tm, n // tn),
        in_specs=[
            pl.BlockSpec((tm, d), lambda i, j: (i, 0)),
            pl.BlockSpec((d, tn), lambda i, j: (0, j)),
            pl.BlockSpec((tm, LANES), lambda i, j: (i % blocks_per_seq, 0)),
            pl.BlockSpec((tm, LANES), lambda i, j: (i % blocks_per_seq, 0)),
        ],
        out_specs=pl.BlockSpec((tm, tn), lambda i, j: (i, j)),
        out_shape=jax.ShapeDtypeStruct((t, n), BF16),
        scratch_shapes=[pltpu.VMEM((tm, d), BF16)],
        compiler_params=_cparams(("parallel", "arbitrary")),
        name="in_proj",
    )(x2d, w_bf16, cos_t, sin_t)


def _retention_kernel(q_ref, k_ref, v_ref, g_ref, mask_ref, qdec_ref, kdec_ref, cdec_ref,
                      gng_ref, gnb_ref, o_ref, state_ref):
    n = pl.program_id(1)

    @pl.when(n == 0)
    def _():
        state_ref[...] = jnp.zeros_like(state_ref)

    for h in range(RET_HEADS):
        ks = slice(h * RET_DK, (h + 1) * RET_DK)
        vs = slice(h * RET_DV, (h + 1) * RET_DV)
        qh = q_ref[:, ks]
        kh = k_ref[:, ks]
        vh = v_ref[:, vs]
        s = lax.dot_general(qh, kh, (((1,), (1,)), ((), ())), preferred_element_type=F32)
        s = s * mask_ref[h]
        intra = jnp.dot(s.astype(BF16), vh, preferred_element_type=F32)
        st = state_ref[h]
        cross = jnp.dot(qh, st.astype(BF16), preferred_element_type=F32) * qdec_ref[h]
        vd = (vh.astype(F32) * kdec_ref[h]).astype(BF16)
        kv = jnp.dot(kh.T, vd, preferred_element_type=F32)
        state_ref[h] = cdec_ref[h] * st + kv
        y = _layernorm(intra + cross, gng_ref[:, vs], gnb_ref[:, vs])
        gg = g_ref[:, vs].astype(F32)
        o_ref[:, vs] = (gg * _sigmoid(gg) * y).astype(o_ref.dtype)


def _retention(proj, tables, gn_g, gn_b, *, batch, seq):
    t = proj.shape[0]
    c = RET_CHUNK
    n_chunks = seq // c
    q_w = RET_HEADS * RET_DK
    v_w = RET_HEADS * RET_DV
    mask, qdec, kdec, cdec = tables
    row = lambda b, n: b * n_chunks + n
    const3 = lambda b, n: (0, 0, 0)
    const2 = lambda b, n: (0, 0)
    return pl.pallas_call(
        _retention_kernel,
        grid=(batch, n_chunks),
        in_specs=[
            pl.BlockSpec((c, q_w), lambda b, n: (row(b, n), 0)),
            pl.BlockSpec((c, q_w), lambda b, n: (row(b, n), 1)),
            pl.BlockSpec((c, v_w), lambda b, n: (row(b, n), 2 * q_w // v_w)),
            pl.BlockSpec((c, v_w), lambda b, n: (row(b, n), 2 * q_w // v_w + 1)),
            pl.BlockSpec(mask.shape, const3),
            pl.BlockSpec(qdec.shape, const3),
            pl.BlockSpec(kdec.shape, const3),
            pl.BlockSpec(cdec.shape, const3),
            pl.BlockSpec((1, v_w), const2),
            pl.BlockSpec((1, v_w), const2),
        ],
        out_specs=pl.BlockSpec((c, v_w), lambda b, n: (row(b, n), 0)),
        out_shape=jax.ShapeDtypeStruct((t, v_w), BF16),
        scratch_shapes=[pltpu.VMEM((RET_HEADS, RET_DK, RET_DV), F32)],
        compiler_params=_cparams(("parallel", "arbitrary")),
        name="retention",
    )(proj, proj, proj, proj, mask, qdec, kdec, cdec, gn_g, gn_b)


def _retention_tables():
    log_gamma = jnp.log(1.0 - jnp.exp2(-5.0 - jnp.arange(RET_HEADS, dtype=F32)))
    idx = jnp.arange(RET_CHUNK, dtype=F32)
    diff = idx[:, None] - idx[None, :]
    scale = RET_DK ** -0.5
    mask = jnp.where(diff[None] >= 0,
                     jnp.exp(jnp.maximum(diff, 0.0)[None] * log_gamma[:, None, None]), 0.0) * scale
    k_decay = jnp.exp((RET_CHUNK - 1 - idx)[None, :] * log_gamma[:, None]) * scale
    q_decay = jnp.exp((idx + 1.0)[None, :] * log_gamma[:, None])
    chunk_decay = jnp.exp(RET_CHUNK * log_gamma)
    bshape = (RET_HEADS, RET_CHUNK, RET_DV)
    qdec = jnp.broadcast_to(q_decay[:, :, None], bshape)
    kdec = jnp.broadcast_to(k_decay[:, :, None], bshape)
    cdec = jnp.broadcast_to(chunk_decay[:, None, None], (RET_HEADS, 1, RET_DV))
    return mask.astype(F32), qdec.astype(F32), kdec.astype(F32), cdec.astype(F32)


def _rope_tables(seq):
    half = RET_DK // 2
    inv_freq = ROPE_BASE ** (-jnp.arange(half, dtype=F32) / half)
    ang = jnp.arange(seq, dtype=F32)[:, None] * inv_freq[None, :]
    cos, sin = jnp.cos(ang), jnp.sin(ang)
    return jnp.concatenate([cos, cos], axis=-1), jnp.concatenate([-sin, sin], axis=-1)


def _conv_kernel(cv_ref, cg_ref, pv_ref, pg_ref, w_ref, b_ref, lng_ref, lnb_ref, o_ref, u_ref, *, rows):
    i = pl.program_id(1)
    tc = cv_ref.shape[0]
    cv = cv_ref[...].astype(F32)
    cg = cg_ref[...].astype(F32)
    pv = pv_ref[...].astype(F32)
    pg = pg_ref[...].astype(F32)
    prev = pv * _sigmoid(pg)
    u_ref[0:CONV_HALO, :] = jnp.where(i == 0, 0.0, prev)
    u_ref[CONV_HALO:CONV_HALO + tc, :] = cv * _sigmoid(cg)
    off = CONV_HALO - (CONV_K - 1)
    for r in range(tc // rows):
        base = r * rows
        acc = jnp.zeros((rows, cv_ref.shape[1]), F32)
        for j in range(CONV_K):
            acc = acc + w_ref[j:j + 1, :] * u_ref[base + off + j:base + off + j + rows, :]
        y = _layernorm(acc + b_ref[...], lng_ref[...], lnb_ref[...])
        o_ref[base:base + rows, :] = (y * _sigmoid(y)).astype(o_ref.dtype)


def _conv_branch(proj, w_dw, b_dw, ln_g, ln_b, *, batch, seq, tc, col0, rows=16):
    t = proj.shape[0]
    ch = w_dw.shape[1]
    nblk = seq // tc
    cv_col = col0 // ch
    halo_per_blk = tc // CONV_HALO
    cur = lambda b, i: b * nblk + i
    prev = lambda b, i: jnp.maximum((b * nblk + i) * halo_per_blk - 1, 0)
    const = lambda b, i: (0, 0)
    return pl.pallas_call(
        functools.partial(_conv_kernel, rows=rows),
        grid=(batch, nblk),
        in_specs=[
            pl.BlockSpec((tc, ch), lambda b, i: (cur(b, i), cv_col)),
            pl.BlockSpec((tc, ch), lambda b, i: (cur(b, i), cv_col + 1)),
            pl.BlockSpec((CONV_HALO, ch), lambda b, i: (prev(b, i), cv_col)),
            pl.BlockSpec((CONV_HALO, ch), lambda b, i: (prev(b, i), cv_col + 1)),
            pl.BlockSpec((CONV_K, ch), const),
            pl.BlockSpec((1, ch), const),
            pl.BlockSpec((1, ch), const),
            pl.BlockSpec((1, ch), const),
        ],
        out_specs=pl.BlockSpec((tc, ch), lambda b, i: (cur(b, i), 0)),
        out_shape=jax.ShapeDtypeStruct((t, ch), BF16),
        scratch_shapes=[pltpu.VMEM((CONV_HALO + tc, ch), F32)],
        compiler_params=_cparams(("parallel", "arbitrary")),
        name="conv_branch",
    )(proj, proj, proj, proj, w_dw, b_dw, ln_g, ln_b)


def _merge_kernel(conv_ref, ret_ref, gc_ref, gr_ref, x_ref, wc_ref, wr_ref, wo_ref, lng_ref, lnb_ref,
                  o_ref, os_ref, *, alpha):
    y_conv = jnp.dot(conv_ref[...], wc_ref[...], preferred_element_type=F32)
    y_ret = jnp.dot(ret_ref[...], wr_ref[...], preferred_element_type=F32)
    h = _sigmoid(gc_ref[...].astype(F32)) * y_conv + _sigmoid(gr_ref[...].astype(F32)) * y_ret
    mix = jnp.dot(h.astype(BF16), wo_ref[...], preferred_element_type=F32)
    y = _layernorm(alpha * x_ref[...] + mix, lng_ref[...], lnb_ref[...])
    o_ref[...] = y
    _store_slabs(os_ref, y)


def _merge(conv_o, ret_o, proj, x2d, wc, wr, wo, ln_g, ln_b, *, tm, gate_col0, alpha):
    t, d = x2d.shape
    ch = conv_o.shape[1]
    v_w = ret_o.shape[1]
    gcol = gate_col0 // d
    const = lambda i: (0, 0)
    resident = functools.partial(pl.BlockSpec, index_map=const, pipeline_mode=pl.Buffered(1))
    return pl.pallas_call(
        functools.partial(_merge_kernel, alpha=alpha),
        grid=(t // tm,),
        in_specs=[
            pl.BlockSpec((tm, ch), lambda i: (i, 0)),
            pl.BlockSpec((tm, v_w), lambda i: (i, 0)),
            pl.BlockSpec((tm, d), lambda i: (i, gcol)),
            pl.BlockSpec((tm, d), lambda i: (i, gcol + 1)),
            pl.BlockSpec((tm, d), lambda i: (i, 0)),
            resident(wc.shape),
            resident(wr.shape),
            resident(wo.shape),
            pl.BlockSpec((1, d), const),
            pl.BlockSpec((1, d), const),
        ],
        out_specs=[pl.BlockSpec((tm, d), lambda i: (i, 0)),
                   pl.BlockSpec((tm * (d // LANES), LANES), lambda i: (i, 0))],
        out_shape=[jax.ShapeDtypeStruct((t, d), F32),
                   jax.ShapeDtypeStruct((t * (d // LANES), LANES), F32)],
        compiler_params=_cparams(("parallel",)),
        name="merge",
    )(conv_o, ret_o, proj, proj, x2d, wc, wr, wo, ln_g, ln_b)


def _router_kernel(x_ref, w_ref, b_ref, tri_ref, idx_ref, gate_ref, rank_ref, cnt_ref, carry_ref):
    i = pl.program_id(0)

    @pl.when(i == 0)
    def _():
        carry_ref[...] = jnp.zeros_like(carry_ref)

    tb = x_ref.shape[0]
    n_e = w_ref.shape[1]
    logits = jnp.dot(x_ref[...], w_ref[...], preferred_element_type=F32,
                     precision=lax.Precision.HIGHEST) + b_ref[...]
    lane_e = lax.broadcasted_iota(jnp.int32, (tb, n_e), 1).astype(F32)
    work = logits
    vals, idxs = [], []
    for _ in range(TOP_K):
        m = jnp.max(work, axis=-1, keepdims=True)
        sel = jnp.min(jnp.where(work == m, lane_e, float(n_e)), axis=-1, keepdims=True)
        vals.append(m)
        idxs.append(sel)
        work = jnp.where(lane_e == sel, -jnp.inf, work)
    exps = [jnp.exp(v - vals[0]) for v in vals]
    denom = exps[0]
    for e in exps[1:]:
        denom = denom + e
    multihot = jnp.zeros((tb, n_e), F32)
    for sel in idxs:
        multihot = multihot + (lane_e == sel).astype(F32)
    before = jnp.dot(tri_ref[...], multihot.astype(BF16), preferred_element_type=F32) + carry_ref[...]
    carry_ref[...] = carry_ref[...] + jnp.sum(multihot, axis=0, keepdims=True)
    cnt_ref[...] = carry_ref[...].astype(jnp.int32)

    lane_o = lax.broadcasted_iota(jnp.int32, (tb, LANES), 1)
    idx_o = jnp.zeros((tb, LANES), jnp.int32)
    gate_o = jnp.zeros((tb, LANES), F32)
    rank_o = jnp.zeros((tb, LANES), jnp.int32)
    for k in range(TOP_K):
        rk = jnp.sum(jnp.where(lane_e == idxs[k], before, 0.0), axis=-1, keepdims=True).astype(jnp.int32)
        idx_o = jnp.where(lane_o == k, idxs[k].astype(jnp.int32), idx_o)
        gate_o = jnp.where(lane_o == k, exps[k] / denom, gate_o)
        rank_o = jnp.where(lane_o == k, rk, rank_o)
    idx_ref[...] = idx_o
    gate_ref[...] = gate_o
    rank_ref[...] = rank_o


def _router(x1, w_router, b_router, *, tb):
    t, d = x1.shape
    n_e = w_router.shape[1]
    tri = (jnp.arange(tb)[:, None] > jnp.arange(tb)[None, :]).astype(BF16)
    const = lambda i: (0, 0)
    row = lambda i: (i, 0)
    return pl.pallas_call(
        _router_kernel,
        grid=(t // tb,),
        in_specs=[
            pl.BlockSpec((tb, d), row),
            pl.BlockSpec((d, n_e), const),
            pl.BlockSpec((1, n_e), const),
            pl.BlockSpec((tb, tb), const),
        ],
        out_specs=[
            pl.BlockSpec((tb, LANES), row),
            pl.BlockSpec((tb, LANES), row),
            pl.BlockSpec((tb, LANES), row),
            pl.BlockSpec((1, n_e), const),
        ],
        out_shape=[
            jax.ShapeDtypeStruct((t, LANES), jnp.int32),
            jax.ShapeDtypeStruct((t, LANES), F32),
            jax.ShapeDtypeStruct((t, LANES), jnp.int32),
            jax.ShapeDtypeStruct((1, n_e), jnp.int32),
        ],
        scratch_shapes=[pltpu.VMEM((1, n_e), F32)],
        compiler_params=_cparams(("arbitrary",)),
        name="router",
    )(x1, w_router, b_router, tri)


def _dispatch_kernel(zstart_ref, zflag_ref, tail_ref, pos_hbm, x_ref, xs_hbm, pos_smem, zero_ref, sem_idx,
                     sem_row, *, tb, tm, zrows, sp):
    i = pl.program_id(0)
    n_e = zstart_ref.shape[0]

    def slab(ref, tok, n_tok):
        return ref.at[pl.ds(pl.multiple_of(tok * sp, sp), n_tok * sp)]

    @pl.when(i == 0)
    def _():
        zero_ref[...] = jnp.zeros_like(zero_ref)
        pieces = tm // zrows

        def zcopy(e, p):
            return pltpu.make_async_copy(zero_ref, slab(xs_hbm, zstart_ref[e] + p * zrows, zrows), sem_row)

        def start(e, c):
            @pl.when(zflag_ref[e] > 0)
            def _():
                for p in range(pieces):
                    zcopy(e, p).start()
            return c

        def wait(e, c):
            @pl.when(zflag_ref[e] > 0)
            def _():
                for p in range(pieces):
                    zcopy(e, p).wait()
            return c

        lax.fori_loop(0, n_e, start, 0)
        lax.fori_loop(0, n_e, wait, 0)

        def tail(blk, c):
            for p in range(pieces):
                cp = pltpu.make_async_copy(zero_ref, slab(xs_hbm, blk * tm + p * zrows, zrows), sem_row)
                cp.start()
                cp.wait()
            return c

        lax.fori_loop(tail_ref[0], xs_hbm.shape[0] // (sp * tm), tail, 0)

    idx_copy = pltpu.make_async_copy(pos_hbm.at[pl.ds(i * (tb * TOP_K), tb * TOP_K)], pos_smem, sem_idx)
    idx_copy.start()
    idx_copy.wait()

    def issue(r, c):
        src = x_ref.at[pl.ds(pl.multiple_of(r * sp, sp), sp)]
        for k in range(TOP_K):
            pltpu.make_async_copy(src, slab(xs_hbm, pos_smem[r * TOP_K + k], 1), sem_row).start(priority=k % 2)
        return c

    lax.fori_loop(0, tb, issue, 0, unroll=4)
    for k in range(TOP_K):
        pltpu.make_async_copy(x_ref, slab(xs_hbm, 0, tb), sem_row).wait()


def _dispatch(x1s, pos_flat, zstart, zflag, tail_blk, *, n_tok, n_pad, tb, tm, zrows):
    sp = x1s.shape[0] // n_tok
    return pl.pallas_call(
        functools.partial(_dispatch_kernel, tb=tb, tm=tm, zrows=zrows, sp=sp),
        grid_spec=pltpu.PrefetchScalarGridSpec(
            num_scalar_prefetch=3,
            grid=(n_tok // tb,),
            in_specs=[pl.BlockSpec(memory_space=pl.ANY),
                      pl.BlockSpec((tb * sp, LANES), lambda i, zs, zf, tl: (i, 0))],
            out_specs=pl.BlockSpec(memory_space=pl.ANY),
            scratch_shapes=[
                pltpu.SMEM((tb * TOP_K,), jnp.int32),
                pltpu.VMEM((zrows * sp, LANES), x1s.dtype),
                pltpu.SemaphoreType.DMA,
                pltpu.SemaphoreType.DMA,
            ],
        ),
        out_shape=jax.ShapeDtypeStruct((n_pad * sp, LANES), x1s.dtype),
        compiler_params=_cparams(("arbitrary",)),
        name="dispatch",
    )(zstart, zflag, tail_blk, pos_flat, x1s)


def _moe_kernel(be_ref, nv_ref, last_ref, x_ref, wg_ref, wu_ref, wd_ref, bg_ref, bu_ref, bd_ref, o_ref,
                xb_ref, acc_ref):
    b = pl.program_id(0)
    f = pl.program_id(1)
    n_f = pl.num_programs(1)

    @pl.when(nv_ref[b] > 0)
    def _():
        tm, d = xb_ref.shape
        s_per = d // LANES

        @pl.when(f == 0)
        def _():
            for s in range(s_per):
                xb_ref[:, s * LANES:(s + 1) * LANES] = _load_slab_cols(x_ref, s, tm, s_per).astype(BF16)

        xb = xb_ref[...]
        gate = jnp.dot(xb, wg_ref[...], preferred_element_type=F32) + bg_ref[...]
        up = jnp.dot(xb, wu_ref[...], preferred_element_type=F32) + bu_ref[...]
        gate = jnp.minimum(gate, SWIGLU_LIMIT)
        up = jnp.clip(up, -SWIGLU_LIMIT, SWIGLU_LIMIT)
        hdn = (up + 1.0) * (gate * _sigmoid(SWIGLU_ALPHA * gate))
        part = jnp.dot(hdn.astype(BF16), wd_ref[...], preferred_element_type=F32)

        @pl.when(f == 0)
        def _():
            acc_ref[...] = part

        @pl.when(f > 0)
        def _():
            acc_ref[...] += part

        @pl.when(f == n_f - 1)
        def _():
            _store_slabs(o_ref, acc_ref[...] + bd_ref[...])

    @pl.when(jnp.logical_and(nv_ref[b] == 0, f == n_f - 1))
    def _():
        o_ref[...] = jnp.zeros_like(o_ref)


def _moe_ffn(xs, block_e, nvalid, last_blk, w_gate_up, b_gate_up, w_down, b_down, *, tm, tf):
    n_e, d, two_ff = w_gate_up.shape
    s_per = d // LANES
    n_pad = xs.shape[0] // s_per
    d_ff = two_ff // 2
    n_f = d_ff // tf
    n_blocks = n_pad // tm

    def blk(b, last):
        return jnp.minimum(b, last[0])

    def ftile(b, f, nv):
        return jnp.where(nv[b] > 0, f, n_f - 1)

    return pl.pallas_call(
        _moe_kernel,
        grid_spec=pltpu.PrefetchScalarGridSpec(
            num_scalar_prefetch=3,
            grid=(n_blocks, n_f),
            in_specs=[
                pl.BlockSpec((tm * s_per, LANES), lambda b, f, be, nv, last: (blk(b, last), 0)),
                pl.BlockSpec((None, d, tf), lambda b, f, be, nv, last: (be[b], 0, ftile(b, f, nv))),
                pl.BlockSpec((None, d, tf), lambda b, f, be, nv, last: (be[b], 0, n_f + ftile(b, f, nv))),
                pl.BlockSpec((None, tf, d), lambda b, f, be, nv, last: (be[b], ftile(b, f, nv), 0)),
                pl.BlockSpec((None, 1, tf), lambda b, f, be, nv, last: (be[b], 0, ftile(b, f, nv))),
                pl.BlockSpec((None, 1, tf), lambda b, f, be, nv, last: (be[b], 0, n_f + ftile(b, f, nv))),
                pl.BlockSpec((None, 1, d), lambda b, f, be, nv, last: (be[b], 0, 0)),
            ],
            out_specs=pl.BlockSpec((tm * s_per, LANES), lambda b, f, be, nv, last: (b, 0)),
            scratch_shapes=[pltpu.VMEM((tm, d), BF16), pltpu.VMEM((tm, d), F32)],
        ),
        out_shape=jax.ShapeDtypeStruct((n_pad * s_per, LANES), F32),
        compiler_params=_cparams(("arbitrary", "arbitrary")),
        name="moe_ffn",
    )(block_e, nvalid, last_blk, xs, w_gate_up, w_gate_up, w_down,
      b_gate_up.reshape(n_e, 1, two_ff), b_gate_up.reshape(n_e, 1, two_ff), b_down.reshape(n_e, 1, d))


def _combine_kernel(pos_hbm, yb_hbm, x_ref, gate_ref, lng_ref, lnb_ref, o_ref, pos_smem, buf_ref, ffn_ref,
                    sem_idx, sem_row, *, tb, alpha):
    i = pl.program_id(0)
    s_per = x_ref.shape[1] // LANES
    idx_copy = pltpu.make_async_copy(pos_hbm.at[pl.ds(i * (tb * TOP_K), tb * TOP_K)], pos_smem, sem_idx)
    idx_copy.start()
    idx_copy.wait()

    def issue(r, c):
        for k in range(TOP_K):
            src = pl.multiple_of(pos_smem[r * TOP_K + k] * s_per, s_per)
            dst = pl.multiple_of(r * s_per, s_per)
            pltpu.make_async_copy(yb_hbm.at[pl.ds(src, s_per)], buf_ref.at[k, pl.ds(dst, s_per)],
                                  sem_row).start(priority=k % 2)
        return c

    lax.fori_loop(0, tb, issue, 0, unroll=4)
    for k in range(TOP_K):
        pltpu.make_async_copy(yb_hbm.at[pl.ds(0, tb * s_per)], buf_ref.at[k], sem_row).wait()

    for s in range(s_per):
        piece = gate_ref[:, 0:1] * _load_slab_cols(buf_ref.at[0], s, tb, s_per)
        for k in range(1, TOP_K):
            piece = piece + gate_ref[:, k:k + 1] * _load_slab_cols(buf_ref.at[k], s, tb, s_per)
        ffn_ref[:, s * LANES:(s + 1) * LANES] = piece
    o_ref[...] = _layernorm(alpha * x_ref[...] + ffn_ref[...], lng_ref[...], lnb_ref[...])


def _combine(pos_flat, yb, x1, gates, ln_g, ln_b, *, tb, alpha):
    t, d = x1.shape
    s_per = d // LANES
    const = lambda i: (0, 0)
    row = lambda i: (i, 0)
    return pl.pallas_call(
        functools.partial(_combine_kernel, tb=tb, alpha=alpha),
        grid=(t // tb,),
        in_specs=[
            pl.BlockSpec(memory_space=pl.ANY),
            pl.BlockSpec(memory_space=pl.ANY),
            pl.BlockSpec((tb, d), row),
            pl.BlockSpec((tb, LANES), row),
            pl.BlockSpec((1, d), const),
            pl.BlockSpec((1, d), const),
        ],
        out_specs=pl.BlockSpec((tb, d), row),
        out_shape=jax.ShapeDtypeStruct((t, d), F32),
        scratch_shapes=[
            pltpu.SMEM((tb * TOP_K,), jnp.int32),
            pltpu.VMEM((TOP_K, tb * s_per, LANES), yb.dtype),
            pltpu.VMEM((tb, d), F32),
            pltpu.SemaphoreType.DMA,
            pltpu.SemaphoreType.DMA,
        ],
        compiler_params=_cparams(("arbitrary",)),
        name="combine",
    )(pos_flat, yb, x1, gates, ln_g, ln_b)


def _tiles(batch, seq):
    t = batch * seq
    return dict(
        proj_tm=min(1024, seq), proj_tn=512,
        conv_tc=min(256, seq),
        merge_tm=min(256, t),
        router_tb=min(512, t),
        dispatch_tb=min(512, t // TOP_K),
        moe_tm=min(512, max(128, t // 8)), moe_tf=512,
        combine_tb=min(256, t),
    )


def _layer(x2d, p, *, batch, seq, depth, cfg):
    t, d = x2d.shape
    alpha = (2 * depth) ** 0.25
    q_w = RET_HEADS * RET_DK
    v_w = RET_HEADS * RET_DV
    ch = p["w_dw"].shape[1]
    n_e = p["w_router"].shape[1]
    row2 = lambda a: a.reshape(1, -1)

    cos_t, sin_t = _rope_tables(seq)
    proj = _in_proj(x2d, p["w_in"].astype(BF16), cos_t, sin_t, seq=seq,
                    tm=cfg["proj_tm"], tn=cfg["proj_tn"], rope_cols=2 * q_w)
    ret_o = _retention(proj, _retention_tables(), row2(p["ret_gn_g"]), row2(p["ret_gn_b"]),
                       batch=batch, seq=seq)
    conv_o = _conv_branch(proj, p["w_dw"], row2(p["b_dw"]), row2(p["conv_ln_g"]), row2(p["conv_ln_b"]),
                          batch=batch, seq=seq, tc=cfg["conv_tc"], col0=2 * q_w + 2 * v_w)
    x1, x1s = _merge(conv_o, ret_o, proj, x2d, p["w_conv_out"].astype(BF16), p["w_ret_out"].astype(BF16),
                     p["w_o"].astype(BF16), row2(p["ln1_g"]), row2(p["ln1_b"]),
                     tm=cfg["merge_tm"], gate_col0=2 * q_w + 2 * v_w + 2 * ch, alpha=alpha)

    idx_o, gate_o, rank_o, counts = _router(x1, p["w_router"], row2(p["b_router"]), tb=cfg["router_tb"])

    tm = cfg["moe_tm"]
    counts = counts[0]
    padded = ((counts + tm - 1) // tm) * tm
    pad_end = jnp.cumsum(padded)
    pad_start = pad_end - padded
    pos = pad_start[idx_o[:, :TOP_K]] + rank_o[:, :TOP_K]
    pos_flat = pos.reshape(-1).astype(jnp.int32)
    n_pad = t * TOP_K + n_e * tm
    n_blocks = n_pad // tm
    blk_row0 = jnp.arange(n_blocks, dtype=jnp.int32) * tm
    block_e = jnp.minimum(jnp.searchsorted(pad_end, blk_row0, side="right"), n_e - 1).astype(jnp.int32)
    nvalid = jnp.clip(counts[block_e] - (blk_row0 - pad_start[block_e]), 0, tm).astype(jnp.int32)
    last_blk = (pad_end[-1:] // tm - 1).astype(jnp.int32)
    zstart = (pad_end - tm).astype(jnp.int32)
    zflag = (padded > 0).astype(jnp.int32)

    xs = _dispatch(x1s, pos_flat, zstart, zflag, last_blk + 1, n_tok=t, n_pad=n_pad, tb=cfg["dispatch_tb"],
                   tm=tm, zrows=min(256, tm))
    yb = _moe_ffn(xs, block_e, nvalid, last_blk, p["w_gate_up"].astype(BF16), p["b_gate_up"],
                  p["w_down"].astype(BF16), p["b_down"], tm=tm, tf=cfg["moe_tf"])
    return _combine(pos_flat, yb, x1, gate_o, row2(p["ln2_g"]), row2(p["ln2_b"]),
                    tb=cfg["combine_tb"], alpha=alpha)


_PARAM_NAMES = ("w_in", "w_dw", "b_dw", "conv_ln_g", "conv_ln_b", "w_conv_out", "ret_gn_g", "ret_gn_b",
                "w_ret_out", "w_o", "ln1_g", "ln1_b", "w_router", "b_router", "w_gate_up", "b_gate_up",
                "w_down", "b_down", "ln2_g", "ln2_b")


def _forward(x, params, cfg=None):
    batch, seq, d = x.shape
    depth = params[0].shape[0]
    cfg = cfg or _tiles(batch, seq)
    x2d = x.reshape(batch * seq, d)
    for l in range(depth):
        p = {name: w[l] for name, w in zip(_PARAM_NAMES, params)}
        x2d = _layer(x2d, p, batch=batch, seq=seq, depth=depth, cfg=cfg)
    return x2d.reshape(batch, seq, d)


def kernel(x, w_in, w_dw, b_dw, conv_ln_g, conv_ln_b, w_conv_out, ret_gn_g, ret_gn_b, w_ret_out, w_o,
           ln1_g, ln1_b, w_router, b_router, w_gate_up, b_gate_up, w_down, b_down, ln2_g, ln2_b):
    return _forward(x, (w_in, w_dw, b_dw, conv_ln_g, conv_ln_b, w_conv_out, ret_gn_g, ret_gn_b, w_ret_out,
                        w_o, ln1_g, ln1_b, w_router, b_router, w_gate_up, b_gate_up, w_down, b_down,
                        ln2_g, ln2_b))
```

```python
import functools

import jax
import jax.numpy as jnp
import numpy as np
from jax import lax
from jax.experimental import pallas as pl
from jax.experimental.pallas import tpu as pltpu

RET_HEADS = 8
RET_DK = 128
RET_DV = 256
RET_CHUNK = 128
ROPE_BASE = 10000.0
CONV_K = 31
TOP_K = 4
SWIGLU_LIMIT = 7.0
SWIGLU_ALPHA = 1.702
LN_EPS = 1e-5

LANES = 128
SUBLANES = 8
CONV_HALO = 32
VMEM_LIMIT = 56 * 1024 * 1024

F32 = jnp.float32
BF16 = jnp.bfloat16


def _cparams(sem):
    return pltpu.CompilerParams(dimension_semantics=sem, vmem_limit_bytes=VMEM_LIMIT)


def _layernorm(x, g, b):
    mu = jnp.mean(x, axis=-1, keepdims=True)
    xc = x - mu
    var = jnp.mean(xc * xc, axis=-1, keepdims=True)
    return xc * lax.rsqrt(var + LN_EPS) * g + b


def _sigmoid(x):
    return 1.0 / (1.0 + jnp.exp(-x))


def _store_slabs(slab_ref, y):
    rows, d = y.shape
    s_per = d // LANES
    for s in range(s_per):
        slab_ref[pl.ds(s, rows, stride=s_per), :] = y[:, s * LANES:(s + 1) * LANES].astype(slab_ref.dtype)


def _load_slab_cols(slab_ref, s, rows, s_per):
    return slab_ref[pl.ds(s, rows, stride=s_per), :]


def _in_proj_kernel(x_ref, w_ref, cos_ref, sin_ref, o_ref, xb_ref, *, n_rope_tiles):
    j = pl.program_id(1)

    @pl.when(j == 0)
    def _():
        xb_ref[...] = x_ref[...].astype(BF16)

    acc = jnp.dot(xb_ref[...], w_ref[...], preferred_element_type=F32)

    @pl.when(j < n_rope_tiles)
    def _():
        cos = cos_ref[...]
        sin = sin_ref[...]
        for g in range(acc.shape[1] // LANES):
            a = acc[:, g * LANES:(g + 1) * LANES]
            r = a * cos + pltpu.roll(a, LANES // 2, 1) * sin
            o_ref[:, g * LANES:(g + 1) * LANES] = r.astype(o_ref.dtype)

    @pl.when(j >= n_rope_tiles)
    def _():
        o_ref[...] = acc.astype(o_ref.dtype)


def _in_proj(x2d, w_bf16, cos_t, sin_t, *, seq, tm, tn, rope_cols):
    t, d = x2d.shape
    n = w_bf16.shape[1]
    blocks_per_seq = seq // tm
    return pl.pallas_call(
        functools.partial(_in_proj_kernel, n_rope_tiles=rope_cols // tn),
        grid=(t // tm, n // tn),
        in_specs=[
            pl.BlockSpec((tm, d), lambda i, j: (i, 0)),
            pl.BlockSpec((d, tn), lambda i, j: (0, j)),
            pl.BlockSpec((tm, LANES), lambda i, j: (i % blocks_per_seq, 0)),
            pl.BlockSpec((tm, LANES), lambda i, j: (i % blocks_per_seq, 0)),
        ],
        out_specs=pl.BlockSpec((tm, tn), lambda i, j: (i, j)),
        out_shape=jax.ShapeDtypeStruct((t, n), BF16),
        scratch_shapes=[pltpu.VMEM((tm, d), BF16)],
        compiler_params=_cparams(("parallel", "arbitrary")),
        name="in_proj",
    )(x2d, w_bf16, cos_t, sin_t)


def _retention_kernel(q_ref, k_ref, v_ref, g_ref, mask_ref, qdec_ref, kdec_ref, cdec_ref,
                      gng_ref, gnb_ref, o_ref, state_ref):
    n = pl.program_id(1)

    @pl.when(n == 0)
    def _():
        state_ref[...] = jnp.zeros_like(state_ref)

    for h in range(RET_HEADS):
        ks = slice(h * RET_DK, (h + 1) * RET_DK)
        vs = slice(h * RET_DV, (h + 1) * RET_DV)
        qh = q_ref[:, ks]
        kh = k_ref[:, ks]
        vh = v_ref[:, vs]
        s = lax.dot_general(qh, kh, (((1,), (1,)), ((), ())), preferred_element_type=F32)
        s = s * mask_ref[h]
        intra = jnp.dot(s.astype(BF16), vh, preferred_element_type=F32)
        st = state_ref[h]
        cross = jnp.dot(qh, st.astype(BF16), preferred_element_type=F32) * qdec_ref[h]
        vd = (vh.astype(F32) * kdec_ref[h]).astype(BF16)
        kv = jnp.dot(kh.T, vd, preferred_element_type=F32)
        state_ref[h] = cdec_ref[h] * st + kv
        y = _layernorm(intra + cross, gng_ref[:, vs], gnb_ref[:, vs])
        gg = g_ref[:, vs].astype(F32)
        o_ref[:, vs] = (gg * _sigmoid(gg) * y).astype(o_ref.dtype)


def _retention(proj, tables, gn_g, gn_b, *, batch, seq):
    t = proj.shape[0]
    c = RET_CHUNK
    n_chunks = seq // c
    q_w = RET_HEADS * RET_DK
    v_w = RET_HEADS * RET_DV
    mask, qdec, kdec, cdec = tables
    row = lambda b, n: b * n_chunks + n
    const3 = lambda b, n: (0, 0, 0)
    const2 = lambda b, n: (0, 0)
    return pl.pallas_call(
        _retention_kernel,
        grid=(batch, n_chunks),
        in_specs=[
            pl.BlockSpec((c, q_w), lambda b, n: (row(b, n), 0)),
            pl.BlockSpec((c, q_w), lambda b, n: (row(b, n), 1)),
            pl.BlockSpec((c, v_w), lambda b, n: (row(b, n), 2 * q_w // v_w)),
            pl.BlockSpec((c, v_w), lambda b, n: (row(b, n), 2 * q_w // v_w + 1)),
            pl.BlockSpec(mask.shape, const3),
            pl.BlockSpec(qdec.shape, const3),
            pl.BlockSpec(kdec.shape, const3),
            pl.BlockSpec(cdec.shape, const3),
            pl.BlockSpec((1, v_w), const2),
            pl.BlockSpec((1, v_w), const2),
        ],
        out_specs=pl.BlockSpec((c, v_w), lambda b, n: (row(b, n), 0)),
        out_shape=jax.ShapeDtypeStruct((t, v_w), BF16),
        scratch_shapes=[pltpu.VMEM((RET_HEADS, RET_DK, RET_DV), F32)],
        compiler_params=_cparams(("parallel", "arbitrary")),
        name="retention",
    )(proj, proj, proj, proj, mask, qdec, kdec, cdec, gn_g, gn_b)


def _retention_tables():
    log_gamma = jnp.log(1.0 - jnp.exp2(-5.0 - jnp.arange(RET_HEADS, dtype=F32)))
    idx = jnp.arange(RET_CHUNK, dtype=F32)
    diff = idx[:, None] - idx[None, :]
    scale = RET_DK ** -0.5
    mask = jnp.where(diff[None] >= 0,
                     jnp.exp(jnp.maximum(diff, 0.0)[None] * log_gamma[:, None, None]), 0.0) * scale
    k_decay = jnp.exp((RET_CHUNK - 1 - idx)[None, :] * log_gamma[:, None]) * scale
    q_decay = jnp.exp((idx + 1.0)[None, :] * log_gamma[:, None])
    chunk_decay = jnp.exp(RET_CHUNK * log_gamma)
    bshape = (RET_HEADS, RET_CHUNK, RET_DV)
    qdec = jnp.broadcast_to(q_decay[:, :, None], bshape)
    kdec = jnp.broadcast_to(k_decay[:, :, None], bshape)
    cdec = jnp.broadcast_to(chunk_decay[:, None, None], (RET_HEADS, 1, RET_DV))
    return mask.astype(F32), qdec.astype(F32), kdec.astype(F32), cdec.astype(F32)


def _rope_tables(seq):
    half = RET_DK // 2
    inv_freq = ROPE_BASE ** (-jnp.arange(half, dtype=F32) / half)
    ang = jnp.arange(seq, dtype=F32)[:, None] * inv_freq[None, :]
    cos, sin = jnp.cos(ang), jnp.sin(ang)
    return jnp.concatenate([cos, cos], axis=-1), jnp.concatenate([-sin, sin], axis=-1)


def _conv_kernel(cv_ref, cg_ref, pv_ref, pg_ref, w_ref, b_ref, lng_ref, lnb_ref, o_ref, u_ref, *, rows):
    i = pl.program_id(1)
    tc = cv_ref.shape[0]
    cv = cv_ref[...].astype(F32)
    cg = cg_ref[...].astype(F32)
    pv = pv_ref[...].astype(F32)
    pg = pg_ref[...].astype(F32)
    prev = pv * _sigmoid(pg)
    length = CONV_HALO + tc
    u_ref[0, 0:CONV_HALO, :] = jnp.where(i == 0, 0.0, prev)
    u_ref[0, CONV_HALO:length, :] = cv * _sigmoid(cg)
    for s in range(1, SUBLANES):
        u_ref[s, 0:length - SUBLANES, :] = u_ref[0, s:s + length - SUBLANES, :]
    off = CONV_HALO - (CONV_K - 1)
    for r in range(tc // rows):
        base = r * rows
        acc = jnp.zeros((rows, cv_ref.shape[1]), F32)
        for j in range(CONV_K):
            shift = (off + j) % SUBLANES
            start = base + off + j - shift
            acc = acc + w_ref[j:j + 1, :] * u_ref[shift, start:start + rows, :]
        y = _layernorm(acc + b_ref[...], lng_ref[...], lnb_ref[...])
        o_ref[base:base + rows, :] = (y * _sigmoid(y)).astype(o_ref.dtype)


def _conv_branch(proj, w_dw, b_dw, ln_g, ln_b, *, batch, seq, tc, col0, rows=16):
    t = proj.shape[0]
    ch = w_dw.shape[1]
    nblk = seq // tc
    cv_col = col0 // ch
    halo_per_blk = tc // CONV_HALO
    cur = lambda b, i: b * nblk + i
    prev = lambda b, i: jnp.maximum((b * nblk + i) * halo_per_blk - 1, 0)
    const = lambda b, i: (0, 0)
    return pl.pallas_call(
        functools.partial(_conv_kernel, rows=rows),
        grid=(batch, nblk),
        in_specs=[
            pl.BlockSpec((tc, ch), lambda b, i: (cur(b, i), cv_col)),
            pl.BlockSpec((tc, ch), lambda b, i: (cur(b, i), cv_col + 1)),
            pl.BlockSpec((CONV_HALO, ch), lambda b, i: (prev(b, i), cv_col)),
            pl.BlockSpec((CONV_HALO, ch), lambda b, i: (prev(b, i), cv_col + 1)),
            pl.BlockSpec((CONV_K, ch), const),
            pl.BlockSpec((1, ch), const),
            pl.BlockSpec((1, ch), const),
            pl.BlockSpec((1, ch), const),
        ],
        out_specs=pl.BlockSpec((tc, ch), lambda b, i: (cur(b, i), 0)),
        out_shape=jax.ShapeDtypeStruct((t, ch), BF16),
        scratch_shapes=[pltpu.VMEM((SUBLANES, CONV_HALO + tc, ch), F32)],
        compiler_params=_cparams(("parallel", "arbitrary")),
        name="conv_branch",
    )(proj, proj, proj, proj, w_dw, b_dw, ln_g, ln_b)


def _merge_kernel(conv_ref, ret_ref, gc_ref, gr_ref, x_ref, wc_ref, wr_ref, wo_ref, lng_ref, lnb_ref,
                  o_ref, os_ref, *, alpha):
    y_conv = jnp.dot(conv_ref[...], wc_ref[...], preferred_element_type=F32)
    y_ret = jnp.dot(ret_ref[...], wr_ref[...], preferred_element_type=F32)
    h = _sigmoid(gc_ref[...].astype(F32)) * y_conv + _sigmoid(gr_ref[...].astype(F32)) * y_ret
    mix = jnp.dot(h.astype(BF16), wo_ref[...], preferred_element_type=F32)
    y = _layernorm(alpha * x_ref[...] + mix, lng_ref[...], lnb_ref[...])
    o_ref[...] = y
    _store_slabs(os_ref, y)


def _merge(conv_o, ret_o, proj, x2d, wc, wr, wo, ln_g, ln_b, *, tm, gate_col0, alpha):
    t, d = x2d.shape
    ch = conv_o.shape[1]
    v_w = ret_o.shape[1]
    gcol = gate_col0 // d
    const = lambda i: (0, 0)
    resident = functools.partial(pl.BlockSpec, index_map=const, pipeline_mode=pl.Buffered(1))
    return pl.pallas_call(
        functools.partial(_merge_kernel, alpha=alpha),
        grid=(t // tm,),
        in_specs=[
            pl.BlockSpec((tm, ch), lambda i: (i, 0)),
            pl.BlockSpec((tm, v_w), lambda i: (i, 0)),
            pl.BlockSpec((tm, d), lambda i: (i, gcol)),
            pl.BlockSpec((tm, d), lambda i: (i, gcol + 1)),
            pl.BlockSpec((tm, d), lambda i: (i, 0)),
            resident(wc.shape),
            resident(wr.shape),
            resident(wo.shape),
            pl.BlockSpec((1, d), const),
            pl.BlockSpec((1, d), const),
        ],
        out_specs=[pl.BlockSpec((tm, d), lambda i: (i, 0)),
                   pl.BlockSpec((tm * (d // LANES), LANES), lambda i: (i, 0))],
        out_shape=[jax.ShapeDtypeStruct((t, d), F32),
                   jax.ShapeDtypeStruct((t * (d // LANES), LANES), F32)],
        compiler_params=_cparams(("parallel",)),
        name="merge",
    )(conv_o, ret_o, proj, proj, x2d, wc, wr, wo, ln_g, ln_b)


def _router_kernel(x_ref, w_ref, b_ref, tri_ref, idx_ref, gate_ref, rank_ref, cnt_ref, carry_ref):
    i = pl.program_id(0)

    @pl.when(i == 0)
    def _():
        carry_ref[...] = jnp.zeros_like(carry_ref)

    tb = x_ref.shape[0]
    n_e = w_ref.shape[1]
    logits = jnp.dot(x_ref[...], w_ref[...], preferred_element_type=F32,
                     precision=lax.Precision.HIGHEST) + b_ref[...]
    lane_e = lax.broadcasted_iota(jnp.int32, (tb, n_e), 1).astype(F32)
    work = logits
    vals, idxs = [], []
    for _ in range(TOP_K):
        m = jnp.max(work, axis=-1, keepdims=True)
        sel = jnp.min(jnp.where(work == m, lane_e, float(n_e)), axis=-1, keepdims=True)
        vals.append(m)
        idxs.append(sel)
        work = jnp.where(lane_e == sel, -jnp.inf, work)
    exps = [jnp.exp(v - vals[0]) for v in vals]
    denom = exps[0]
    for e in exps[1:]:
        denom = denom + e
    multihot = jnp.zeros((tb, n_e), F32)
    for sel in idxs:
        multihot = multihot + (lane_e == sel).astype(F32)
    before = jnp.dot(tri_ref[...], multihot.astype(BF16), preferred_element_type=F32) + carry_ref[...]
    carry_ref[...] = carry_ref[...] + jnp.sum(multihot, axis=0, keepdims=True)
    cnt_ref[...] = carry_ref[...].astype(jnp.int32)

    lane_o = lax.broadcasted_iota(jnp.int32, (tb, LANES), 1)
    idx_o = jnp.zeros((tb, LANES), jnp.int32)
    gate_o = jnp.zeros((tb, LANES), F32)
    rank_o = jnp.zeros((tb, LANES), jnp.int32)
    for k in range(TOP_K):
        rk = jnp.sum(jnp.where(lane_e == idxs[k], before, 0.0), axis=-1, keepdims=True).astype(jnp.int32)
        idx_o = jnp.where(lane_o == k, idxs[k].astype(jnp.int32), idx_o)
        gate_o = jnp.where(lane_o == k, exps[k] / denom, gate_o)
        rank_o = jnp.where(lane_o == k, rk, rank_o)
    idx_ref[...] = idx_o
    gate_ref[...] = gate_o
    rank_ref[...] = rank_o


def _router(x1, w_router, b_router, *, tb):
    t, d = x1.shape
    n_e = w_router.shape[1]
    tri = (jnp.arange(tb)[:, None] > jnp.arange(tb)[None, :]).astype(BF16)
    const = lambda i: (0, 0)
    row = lambda i: (i, 0)
    return pl.pallas_call(
        _router_kernel,
        grid=(t // tb,),
        in_specs=[
            pl.BlockSpec((tb, d), row),
            pl.BlockSpec((d, n_e), const),
            pl.BlockSpec((1, n_e), const),
            pl.BlockSpec((tb, tb), const),
        ],
        out_specs=[
            pl.BlockSpec((tb, LANES), row),
            pl.BlockSpec((tb, LANES), row),
            pl.BlockSpec((tb, LANES), row),
            pl.BlockSpec((1, n_e), const),
        ],
        out_shape=[
            jax.ShapeDtypeStruct((t, LANES), jnp.int32),
            jax.ShapeDtypeStruct((t, LANES), F32),
            jax.ShapeDtypeStruct((t, LANES), jnp.int32),
            jax.ShapeDtypeStruct((1, n_e), jnp.int32),
        ],
        scratch_shapes=[pltpu.VMEM((1, n_e), F32)],
        compiler_params=_cparams(("arbitrary",)),
        name="router",
    )(x1, w_router, b_router, tri)


def _dispatch_kernel(zstart_ref, zflag_ref, tail_ref, pos_hbm, x_ref, xs_hbm, pos_smem, zero_ref, sem_idx,
                     sem_row, *, tb, tm, zrows, sp):
    i = pl.program_id(0)
    n_e = zstart_ref.shape[0]

    def slab(ref, tok, n_tok):
        return ref.at[pl.ds(pl.multiple_of(tok * sp, sp), n_tok * sp)]

    @pl.when(i == 0)
    def _():
        zero_ref[...] = jnp.zeros_like(zero_ref)
        pieces = tm // zrows

        def zcopy(e, p):
            return pltpu.make_async_copy(zero_ref, slab(xs_hbm, zstart_ref[e] + p * zrows, zrows), sem_row)

        def start(e, c):
            @pl.when(zflag_ref[e] > 0)
            def _():
                for p in range(pieces):
                    zcopy(e, p).start()
            return c

        def wait(e, c):
            @pl.when(zflag_ref[e] > 0)
            def _():
                for p in range(pieces):
                    zcopy(e, p).wait()
            return c

        lax.fori_loop(0, n_e, start, 0)
        lax.fori_loop(0, n_e, wait, 0)

        def tail(blk, c):
            for p in range(pieces):
                cp = pltpu.make_async_copy(zero_ref, slab(xs_hbm, blk * tm + p * zrows, zrows), sem_row)
                cp.start()
                cp.wait()
            return c

        lax.fori_loop(tail_ref[0], xs_hbm.shape[0] // (sp * tm), tail, 0)

    idx_copy = pltpu.make_async_copy(pos_hbm.at[pl.ds(i * (tb * TOP_K), tb * TOP_K)], pos_smem, sem_idx)
    idx_copy.start()
    idx_copy.wait()

    def issue(r, c):
        src = x_ref.at[pl.ds(pl.multiple_of(r * sp, sp), sp)]
        for k in range(TOP_K):
            pltpu.make_async_copy(src, slab(xs_hbm, pos_smem[r * TOP_K + k], 1), sem_row).start(priority=k % 2)
        return c

    lax.fori_loop(0, tb, issue, 0, unroll=4)
    for k in range(TOP_K):
        pltpu.make_async_copy(x_ref, slab(xs_hbm, 0, tb), sem_row).wait()


def _dispatch(x1s, pos_flat, zstart, zflag, tail_blk, *, n_tok, n_pad, tb, tm, zrows):
    sp = x1s.shape[0] // n_tok
    return pl.pallas_call(
        functools.partial(_dispatch_kernel, tb=tb, tm=tm, zrows=zrows, sp=sp),
        grid_spec=pltpu.PrefetchScalarGridSpec(
            num_scalar_prefetch=3,
            grid=(n_tok // tb,),
            in_specs=[pl.BlockSpec(memory_space=pl.ANY),
                      pl.BlockSpec((tb * sp, LANES), lambda i, zs, zf, tl: (i, 0))],
            out_specs=pl.BlockSpec(memory_space=pl.ANY),
            scratch_shapes=[
                pltpu.SMEM((tb * TOP_K,), jnp.int32),
                pltpu.VMEM((zrows * sp, LANES), x1s.dtype),
                pltpu.SemaphoreType.DMA,
                pltpu.SemaphoreType.DMA,
            ],
        ),
        out_shape=jax.ShapeDtypeStruct((n_pad * sp, LANES), x1s.dtype),
        compiler_params=_cparams(("arbitrary",)),
        name="dispatch",
    )(zstart, zflag, tail_blk, pos_flat, x1s)


def _moe_kernel(be_ref, nv_ref, last_ref, x_ref, wg_ref, wu_ref, wd_ref, bg_ref, bu_ref, bd_ref, o_ref,
                xb_ref, acc_ref):
    b = pl.program_id(0)
    f = pl.program_id(1)
    n_f = pl.num_programs(1)

    @pl.when(nv_ref[b] > 0)
    def _():
        tm, d = xb_ref.shape
        s_per = d // LANES

        @pl.when(f == 0)
        def _():
            for s in range(s_per):
                xb_ref[:, s * LANES:(s + 1) * LANES] = _load_slab_cols(x_ref, s, tm, s_per).astype(BF16)

        xb = xb_ref[...]
        gate = jnp.dot(xb, wg_ref[...].astype(BF16), preferred_element_type=F32) + bg_ref[...]
        up = jnp.dot(xb, wu_ref[...].astype(BF16), preferred_element_type=F32) + bu_ref[...]
        gate = jnp.minimum(gate, SWIGLU_LIMIT)
        up = jnp.clip(up, -SWIGLU_LIMIT, SWIGLU_LIMIT)
        hdn = (up + 1.0) * (gate * _sigmoid(SWIGLU_ALPHA * gate))
        part = jnp.dot(hdn.astype(BF16), wd_ref[...].astype(BF16), preferred_element_type=F32)

        @pl.when(f == 0)
        def _():
            acc_ref[...] = part

        @pl.when(f > 0)
        def _():
            acc_ref[...] += part

        @pl.when(f == n_f - 1)
        def _():
            _store_slabs(o_ref, acc_ref[...] + bd_ref[...])

    @pl.when(jnp.logical_and(nv_ref[b] == 0, f == n_f - 1))
    def _():
        o_ref[...] = jnp.zeros_like(o_ref)


def _moe_ffn(xs, block_e, nvalid, last_blk, w_gate_up, b_gate_up, w_down, b_down, *, tm, tf):
    n_e, d, two_ff = w_gate_up.shape
    s_per = d // LANES
    n_pad = xs.shape[0] // s_per
    d_ff = two_ff // 2
    n_f = d_ff // tf
    n_blocks = n_pad // tm

    def blk(b, last):
        return jnp.minimum(b, last[0])

    def ftile(b, f, nv):
        return jnp.where(nv[b] > 0, f, n_f - 1)

    return pl.pallas_call(
        _moe_kernel,
        grid_spec=pltpu.PrefetchScalarGridSpec(
            num_scalar_prefetch=3,
            grid=(n_blocks, n_f),
            in_specs=[
                pl.BlockSpec((tm * s_per, LANES), lambda b, f, be, nv, last: (blk(b, last), 0)),
                pl.BlockSpec((None, d, tf), lambda b, f, be, nv, last: (be[b], 0, ftile(b, f, nv))),
                pl.BlockSpec((None, d, tf), lambda b, f, be, nv, last: (be[b], 0, n_f + ftile(b, f, nv))),
                pl.BlockSpec((None, tf, d), lambda b, f, be, nv, last: (be[b], ftile(b, f, nv), 0)),
                pl.BlockSpec((None, 1, tf), lambda b, f, be, nv, last: (be[b], 0, ftile(b, f, nv))),
                pl.BlockSpec((None, 1, tf), lambda b, f, be, nv, last: (be[b], 0, n_f + ftile(b, f, nv))),
                pl.BlockSpec((None, 1, d), lambda b, f, be, nv, last: (be[b], 0, 0)),
            ],
            out_specs=pl.BlockSpec((tm * s_per, LANES), lambda b, f, be, nv, last: (b, 0)),
            scratch_shapes=[pltpu.VMEM((tm, d), BF16), pltpu.VMEM((tm, d), F32)],
        ),
        out_shape=jax.ShapeDtypeStruct((n_pad * s_per, LANES), F32),
        compiler_params=_cparams(("arbitrary", "arbitrary")),
        name="moe_ffn",
    )(block_e, nvalid, last_blk, xs, w_gate_up, w_gate_up, w_down,
      b_gate_up.reshape(n_e, 1, two_ff), b_gate_up.reshape(n_e, 1, two_ff), b_down.reshape(n_e, 1, d))


def _combine_kernel(pos_hbm, yb_hbm, x_ref, gate_ref, lng_ref, lnb_ref, o_ref, pos_smem, buf_ref, ffn_ref,
                    sem_idx, sem_row, *, tb, alpha):
    i = pl.program_id(0)
    s_per = x_ref.shape[1] // LANES
    idx_copy = pltpu.make_async_copy(pos_hbm.at[pl.ds(i * (tb * TOP_K), tb * TOP_K)], pos_smem, sem_idx)
    idx_copy.start()
    idx_copy.wait()

    def issue(r, c):
        for k in range(TOP_K):
            src = pl.multiple_of(pos_smem[r * TOP_K + k] * s_per, s_per)
            dst = pl.multiple_of(r * s_per, s_per)
            pltpu.make_async_copy(yb_hbm.at[pl.ds(src, s_per)], buf_ref.at[k, pl.ds(dst, s_per)],
                                  sem_row).start(priority=k % 2)
        return c

    lax.fori_loop(0, tb, issue, 0, unroll=4)
    for k in range(TOP_K):
        pltpu.make_async_copy(yb_hbm.at[pl.ds(0, tb * s_per)], buf_ref.at[k], sem_row).wait()

    for s in range(s_per):
        piece = gate_ref[:, 0:1] * _load_slab_cols(buf_ref.at[0], s, tb, s_per)
        for k in range(1, TOP_K):
            piece = piece + gate_ref[:, k:k + 1] * _load_slab_cols(buf_ref.at[k], s, tb, s_per)
        ffn_ref[:, s * LANES:(s + 1) * LANES] = piece
    o_ref[...] = _layernorm(alpha * x_ref[...] + ffn_ref[...], lng_ref[...], lnb_ref[...])


def _combine(pos_flat, yb, x1, gates, ln_g, ln_b, *, tb, alpha):
    t, d = x1.shape
    s_per = d // LANES
    const = lambda i: (0, 0)
    row = lambda i: (i, 0)
    return pl.pallas_call(
        functools.partial(_combine_kernel, tb=tb, alpha=alpha),
        grid=(t // tb,),
        in_specs=[
            pl.BlockSpec(memory_space=pl.ANY),
            pl.BlockSpec(memory_space=pl.ANY),
            pl.BlockSpec((tb, d), row),
            pl.BlockSpec((tb, LANES), row),
            pl.BlockSpec((1, d), const),
            pl.BlockSpec((1, d), const),
        ],
        out_specs=pl.BlockSpec((tb, d), row),
        out_shape=jax.ShapeDtypeStruct((t, d), F32),
        scratch_shapes=[
            pltpu.SMEM((tb * TOP_K,), jnp.int32),
            pltpu.VMEM((TOP_K, tb * s_per, LANES), yb.dtype),
            pltpu.VMEM((tb, d), F32),
            pltpu.SemaphoreType.DMA,
            pltpu.SemaphoreType.DMA,
        ],
        compiler_params=_cparams(("arbitrary",)),
        name="combine",
    )(pos_flat, yb, x1, gates, ln_g, ln_b)


def _tiles(batch, seq):
    t = batch * seq
    return dict(
        proj_tm=min(1024, seq), proj_tn=512,
        conv_tc=min(256, seq),
        merge_tm=min(256, t),
        router_tb=min(512, t),
        dispatch_tb=min(512, t // TOP_K),
        moe_tm=min(512, max(128, t // 8)), moe_tf=512,
        combine_tb=min(256, t),
    )


def _layer(x2d, p, *, batch, seq, depth, cfg):
    t, d = x2d.shape
    alpha = (2 * depth) ** 0.25
    q_w = RET_HEADS * RET_DK
    v_w = RET_HEADS * RET_DV
    ch = p["w_dw"].shape[1]
    n_e = p["w_router"].shape[1]
    row2 = lambda a: a.reshape(1, -1)

    cos_t, sin_t = _rope_tables(seq)
    proj = _in_proj(x2d, p["w_in"].astype(BF16), cos_t, sin_t, seq=seq,
                    tm=cfg["proj_tm"], tn=cfg["proj_tn"], rope_cols=2 * q_w)
    ret_o = _retention(proj, _retention_tables(), row2(p["ret_gn_g"]), row2(p["ret_gn_b"]),
                       batch=batch, seq=seq)
    conv_o = _conv_branch(proj, p["w_dw"], row2(p["b_dw"]), row2(p["conv_ln_g"]), row2(p["conv_ln_b"]),
                          batch=batch, seq=seq, tc=cfg["conv_tc"], col0=2 * q_w + 2 * v_w)
    x1, x1s = _merge(conv_o, ret_o, proj, x2d, p["w_conv_out"].astype(BF16), p["w_ret_out"].astype(BF16),
                     p["w_o"].astype(BF16), row2(p["ln1_g"]), row2(p["ln1_b"]),
                     tm=cfg["merge_tm"], gate_col0=2 * q_w + 2 * v_w + 2 * ch, alpha=alpha)

    idx_o, gate_o, rank_o, counts = _router(x1, p["w_router"], row2(p["b_router"]), tb=cfg["router_tb"])

    tm = cfg["moe_tm"]
    counts = counts[0]
    padded = ((counts + tm - 1) // tm) * tm
    pad_end = jnp.cumsum(padded)
    pad_start = pad_end - padded
    pos = pad_start[idx_o[:, :TOP_K]] + rank_o[:, :TOP_K]
    pos_flat = pos.reshape(-1).astype(jnp.int32)
    n_pad = t * TOP_K + n_e * tm
    n_blocks = n_pad // tm
    blk_row0 = jnp.arange(n_blocks, dtype=jnp.int32) * tm
    block_e = jnp.minimum(jnp.searchsorted(pad_end, blk_row0, side="right"), n_e - 1).astype(jnp.int32)
    nvalid = jnp.clip(counts[block_e] - (blk_row0 - pad_start[block_e]), 0, tm).astype(jnp.int32)
    last_blk = (pad_end[-1:] // tm - 1).astype(jnp.int32)
    zstart = (pad_end - tm).astype(jnp.int32)
    zflag = (padded > 0).astype(jnp.int32)

    xs = _dispatch(x1s, pos_flat, zstart, zflag, last_blk + 1, n_tok=t, n_pad=n_pad, tb=cfg["dispatch_tb"],
                   tm=tm, zrows=min(256, tm))
    yb = _moe_ffn(xs, block_e, nvalid, last_blk, p["w_gate_up"], p["b_gate_up"], p["w_down"], p["b_down"],
                  tm=tm, tf=cfg["moe_tf"])
    return _combine(pos_flat, yb, x1, gate_o, row2(p["ln2_g"]), row2(p["ln2_b"]),
                    tb=cfg["combine_tb"], alpha=alpha)


_PARAM_NAMES = ("w_in", "w_dw", "b_dw", "conv_ln_g", "conv_ln_b", "w_conv_out", "ret_gn_g", "ret_gn_b",
                "w_ret_out", "w_o", "ln1_g", "ln1_b", "w_router", "b_router", "w_gate_up", "b_gate_up",
                "w_down", "b_down", "ln2_g", "ln2_b")


def _forward(x, params, cfg=None):
    batch, seq, d = x.shape
    depth = params[0].shape[0]
    cfg = cfg or _tiles(batch, seq)
    x2d = x.reshape(batch * seq, d)
    for l in range(depth):
        p = {name: w[l] for name, w in zip(_PARAM_NAMES, params)}
        x2d = _layer(x2d, p, batch=batch, seq=seq, depth=depth, cfg=cfg)
    return x2d.reshape(batch, seq, d)


def kernel(x, w_in, w_dw, b_dw, conv_ln_g, conv_ln_b, w_conv_out, ret_gn_g, ret_gn_b, w_ret_out, w_o,
           ln1_g, ln1_b, w_router, b_router, w_gate_up, b_gate_up, w_down, b_down, ln2_g, ln2_b):
    return _forward(x, (w_in, w_dw, b_dw, conv_ln_g, conv_ln_b, w_conv_out, ret_gn_g, ret_gn_b, w_ret_out,
                        w_o, ln1_g, ln1_b, w_router, b_router, w_gate_up, b_gate_up, w_down, b_down,
                        ln2_g, ln2_b))
```

```python
import functools

import jax
import jax.numpy as jnp
import numpy as np
from jax import lax
from jax.experimental import pallas as pl
from jax.experimental.pallas import tpu as pltpu

RET_HEADS = 8
RET_DK = 128
RET_DV = 256
RET_CHUNK = 128
ROPE_BASE = 10000.0
CONV_K = 31
TOP_K = 4
SWIGLU_LIMIT = 7.0
SWIGLU_ALPHA = 1.702
LN_EPS = 1e-5

LANES = 128
SUBLANES = 8
CONV_HALO = 32
VMEM_LIMIT = 56 * 1024 * 1024

F32 = jnp.float32
BF16 = jnp.bfloat16


def _cparams(sem):
    return pltpu.CompilerParams(dimension_semantics=sem, vmem_limit_bytes=VMEM_LIMIT)


def _layernorm(x, g, b):
    mu = jnp.mean(x, axis=-1, keepdims=True)
    xc = x - mu
    var = jnp.mean(xc * xc, axis=-1, keepdims=True)
    return xc * lax.rsqrt(var + LN_EPS) * g + b


def _sigmoid(x):
    return 1.0 / (1.0 + jnp.exp(-x))


def _store_slabs(slab_ref, y):
    rows, d = y.shape
    s_per = d // LANES
    for s in range(s_per):
        slab_ref[pl.ds(s, rows, stride=s_per), :] = y[:, s * LANES:(s + 1) * LANES].astype(slab_ref.dtype)


def _load_slab_cols(slab_ref, s, rows, s_per):
    return slab_ref[pl.ds(s, rows, stride=s_per), :]


def _in_proj_kernel(x_ref, w_ref, cos_ref, sin_ref, o_ref, xb_ref, *, n_rope_tiles):
    j = pl.program_id(1)

    @pl.when(j == 0)
    def _():
        xb_ref[...] = x_ref[...].astype(BF16)

    acc = jnp.dot(xb_ref[...], w_ref[...], preferred_element_type=F32)

    @pl.when(j < n_rope_tiles)
    def _():
        cos = cos_ref[...]
        sin = sin_ref[...]
        for g in range(acc.shape[1] // LANES):
            a = acc[:, g * LANES:(g + 1) * LANES]
            r = a * cos + pltpu.roll(a, LANES // 2, 1) * sin
            o_ref[:, g * LANES:(g + 1) * LANES] = r.astype(o_ref.dtype)

    @pl.when(j >= n_rope_tiles)
    def _():
        o_ref[...] = acc.astype(o_ref.dtype)


def _in_proj(x2d, w_bf16, cos_t, sin_t, *, seq, tm, tn, rope_cols):
    t, d = x2d.shape
    n = w_bf16.shape[1]
    blocks_per_seq = seq // tm
    return pl.pallas_call(
        functools.partial(_in_proj_kernel, n_rope_tiles=rope_cols // tn),
        grid=(t // tm, n // tn),
        in_specs=[
            pl.BlockSpec((tm, d), lambda i, j: (i, 0)),
            pl.BlockSpec((d, tn), lambda i, j: (0, j)),
            pl.BlockSpec((tm, LANES), lambda i, j: (i % blocks_per_seq, 0)),
            pl.BlockSpec((tm, LANES), lambda i, j: (i % blocks_per_seq, 0)),
        ],
        out_specs=pl.BlockSpec((tm, tn), lambda i, j: (i, j)),
        out_shape=jax.ShapeDtypeStruct((t, n), BF16),
        scratch_shapes=[pltpu.VMEM((tm, d), BF16)],
        compiler_params=_cparams(("parallel", "arbitrary")),
        name="in_proj",
    )(x2d, w_bf16, cos_t, sin_t)


def _retention_kernel(q_ref, k_ref, v_ref, g_ref, mask_ref, qdec_ref, kdec_ref, cdec_ref,
                      gng_ref, gnb_ref, o_ref, state_ref):
    n = pl.program_id(1)

    @pl.when(n == 0)
    def _():
        state_ref[...] = jnp.zeros_like(state_ref)

    for h in range(RET_HEADS):
        ks = slice(h * RET_DK, (h + 1) * RET_DK)
        vs = slice(h * RET_DV, (h + 1) * RET_DV)
        qh = q_ref[:, ks]
        kh = k_ref[:, ks]
        vh = v_ref[:, vs]
        s = lax.dot_general(qh, kh, (((1,), (1,)), ((), ())), preferred_element_type=F32)
        s = s * mask_ref[h]
        intra = jnp.dot(s.astype(BF16), vh, preferred_element_type=F32)
        st = state_ref[h]
        cross = jnp.dot(qh, st.astype(BF16), preferred_element_type=F32) * qdec_ref[h]
        vd = (vh.astype(F32) * kdec_ref[h]).astype(BF16)
        kv = jnp.dot(kh.T, vd, preferred_element_type=F32)
        state_ref[h] = cdec_ref[h] * st + kv
        y = _layernorm(intra + cross, gng_ref[:, vs], gnb_ref[:, vs])
        gg = g_ref[:, vs].astype(F32)
        o_ref[:, vs] = (gg * _sigmoid(gg) * y).astype(o_ref.dtype)


def _retention(proj, tables, gn_g, gn_b, *, batch, seq):
    t = proj.shape[0]
    c = RET_CHUNK
    n_chunks = seq // c
    q_w = RET_HEADS * RET_DK
    v_w = RET_HEADS * RET_DV
    mask, qdec, kdec, cdec = tables
    row = lambda b, n: b * n_chunks + n
    const3 = lambda b, n: (0, 0, 0)
    const2 = lambda b, n: (0, 0)
    return pl.pallas_call(
        _retention_kernel,
        grid=(batch, n_chunks),
        in_specs=[
            pl.BlockSpec((c, q_w), lambda b, n: (row(b, n), 0)),
            pl.BlockSpec((c, q_w), lambda b, n: (row(b, n), 1)),
            pl.BlockSpec((c, v_w), lambda b, n: (row(b, n), 2 * q_w // v_w)),
            pl.BlockSpec((c, v_w), lambda b, n: (row(b, n), 2 * q_w // v_w + 1)),
            pl.BlockSpec(mask.shape, const3),
            pl.BlockSpec(qdec.shape, const3),
            pl.BlockSpec(kdec.shape, const3),
            pl.BlockSpec(cdec.shape, const3),
            pl.BlockSpec((1, v_w), const2),
            pl.BlockSpec((1, v_w), const2),
        ],
        out_specs=pl.BlockSpec((c, v_w), lambda b, n: (row(b, n), 0)),
        out_shape=jax.ShapeDtypeStruct((t, v_w), BF16),
        scratch_shapes=[pltpu.VMEM((RET_HEADS, RET_DK, RET_DV), F32)],
        compiler_params=_cparams(("parallel", "arbitrary")),
        name="retention",
    )(proj, proj, proj, proj, mask, qdec, kdec, cdec, gn_g, gn_b)


def _retention_tables():
    f32 = np.float32
    log_gamma = np.log(f32(1.0) - np.exp2(f32(-5.0) - np.arange(RET_HEADS, dtype=f32))).astype(f32)
    idx = np.arange(RET_CHUNK, dtype=f32)
    diff = idx[:, None] - idx[None, :]
    scale = f32(RET_DK ** -0.5)
    mask = np.where(diff[None] >= 0,
                    np.exp(np.maximum(diff, f32(0.0))[None] * log_gamma[:, None, None]), f32(0.0)) * scale
    k_decay = np.exp((RET_CHUNK - 1 - idx)[None, :] * log_gamma[:, None]) * scale
    q_decay = np.exp((idx + f32(1.0))[None, :] * log_gamma[:, None])
    chunk_decay = np.exp(f32(RET_CHUNK) * log_gamma)
    bshape = (RET_HEADS, RET_CHUNK, RET_DV)
    qdec = np.broadcast_to(q_decay[:, :, None], bshape)
    kdec = np.broadcast_to(k_decay[:, :, None], bshape)
    cdec = np.broadcast_to(chunk_decay[:, None, None], (RET_HEADS, 1, RET_DV))
    return tuple(jnp.asarray(np.ascontiguousarray(a, dtype=f32)) for a in (mask, qdec, kdec, cdec))


def _rope_tables(seq):
    f32 = np.float32
    half = RET_DK // 2
    inv_freq = np.power(f32(ROPE_BASE), -np.arange(half, dtype=f32) / f32(half)).astype(f32)
    ang = (np.arange(seq, dtype=f32)[:, None] * inv_freq[None, :]).astype(f32)
    cos, sin = np.cos(ang).astype(f32), np.sin(ang).astype(f32)
    return (jnp.asarray(np.concatenate([cos, cos], axis=-1)), jnp.asarray(np.concatenate([-sin, sin], axis=-1)))


def _conv_kernel(cv_ref, cg_ref, pv_ref, pg_ref, w_ref, b_ref, lng_ref, lnb_ref, o_ref, u_ref, *, rows):
    i = pl.program_id(1)
    tc = cv_ref.shape[0]
    cv = cv_ref[...].astype(F32)
    cg = cg_ref[...].astype(F32)
    pv = pv_ref[...].astype(F32)
    pg = pg_ref[...].astype(F32)
    prev = pv * _sigmoid(pg)
    length = CONV_HALO + tc
    u_ref[0, 0:CONV_HALO, :] = jnp.where(i == 0, 0.0, prev)
    u_ref[0, CONV_HALO:length, :] = cv * _sigmoid(cg)
    for s in range(1, SUBLANES):
        u_ref[s, 0:length - SUBLANES, :] = u_ref[0, s:s + length - SUBLANES, :]
    off = CONV_HALO - (CONV_K - 1)
    for r in range(tc // rows):
        base = r * rows
        acc = jnp.zeros((rows, cv_ref.shape[1]), F32)
        for j in range(CONV_K):
            shift = (off + j) % SUBLANES
            start = base + off + j - shift
            acc = acc + w_ref[j:j + 1, :] * u_ref[shift, start:start + rows, :]
        y = _layernorm(acc + b_ref[...], lng_ref[...], lnb_ref[...])
        o_ref[base:base + rows, :] = (y * _sigmoid(y)).astype(o_ref.dtype)


def _conv_branch(proj, w_dw, b_dw, ln_g, ln_b, *, batch, seq, tc, col0, rows=16):
    t = proj.shape[0]
    ch = w_dw.shape[1]
    nblk = seq // tc
    cv_col = col0 // ch
    halo_per_blk = tc // CONV_HALO
    cur = lambda b, i: b * nblk + i
    prev = lambda b, i: jnp.maximum((b * nblk + i) * halo_per_blk - 1, 0)
    const = lambda b, i: (0, 0)
    return pl.pallas_call(
        functools.partial(_conv_kernel, rows=rows),
        grid=(batch, nblk),
        in_specs=[
            pl.BlockSpec((tc, ch), lambda b, i: (cur(b, i), cv_col)),
            pl.BlockSpec((tc, ch), lambda b, i: (cur(b, i), cv_col + 1)),
            pl.BlockSpec((CONV_HALO, ch), lambda b, i: (prev(b, i), cv_col)),
            pl.BlockSpec((CONV_HALO, ch), lambda b, i: (prev(b, i), cv_col + 1)),
            pl.BlockSpec((CONV_K, ch), const),
            pl.BlockSpec((1, ch), const),
            pl.BlockSpec((1, ch), const),
            pl.BlockSpec((1, ch), const),
        ],
        out_specs=pl.BlockSpec((tc, ch), lambda b, i: (cur(b, i), 0)),
        out_shape=jax.ShapeDtypeStruct((t, ch), BF16),
        scratch_shapes=[pltpu.VMEM((SUBLANES, CONV_HALO + tc, ch), F32)],
        compiler_params=_cparams(("parallel", "arbitrary")),
        name="conv_branch",
    )(proj, proj, proj, proj, w_dw, b_dw, ln_g, ln_b)


def _merge_kernel(conv_ref, ret_ref, gc_ref, gr_ref, x_ref, wc_ref, wr_ref, wo_ref, lng_ref, lnb_ref,
                  o_ref, os_ref, *, alpha):
    y_conv = jnp.dot(conv_ref[...], wc_ref[...], preferred_element_type=F32)
    y_ret = jnp.dot(ret_ref[...], wr_ref[...], preferred_element_type=F32)
    h = _sigmoid(gc_ref[...].astype(F32)) * y_conv + _sigmoid(gr_ref[...].astype(F32)) * y_ret
    mix = jnp.dot(h.astype(BF16), wo_ref[...], preferred_element_type=F32)
    y = _layernorm(alpha * x_ref[...] + mix, lng_ref[...], lnb_ref[...])
    o_ref[...] = y
    _store_slabs(os_ref, y)


def _merge(conv_o, ret_o, proj, x2d, wc, wr, wo, ln_g, ln_b, *, tm, gate_col0, alpha):
    t, d = x2d.shape
    ch = conv_o.shape[1]
    v_w = ret_o.shape[1]
    gcol = gate_col0 // d
    const = lambda i: (0, 0)
    resident = functools.partial(pl.BlockSpec, index_map=const, pipeline_mode=pl.Buffered(1))
    return pl.pallas_call(
        functools.partial(_merge_kernel, alpha=alpha),
        grid=(t // tm,),
        in_specs=[
            pl.BlockSpec((tm, ch), lambda i: (i, 0)),
            pl.BlockSpec((tm, v_w), lambda i: (i, 0)),
            pl.BlockSpec((tm, d), lambda i: (i, gcol)),
            pl.BlockSpec((tm, d), lambda i: (i, gcol + 1)),
            pl.BlockSpec((tm, d), lambda i: (i, 0)),
            resident(wc.shape),
            resident(wr.shape),
            resident(wo.shape),
            pl.BlockSpec((1, d), const),
            pl.BlockSpec((1, d), const),
        ],
        out_specs=[pl.BlockSpec((tm, d), lambda i: (i, 0)),
                   pl.BlockSpec((tm * (d // LANES), LANES), lambda i: (i, 0))],
        out_shape=[jax.ShapeDtypeStruct((t, d), F32),
                   jax.ShapeDtypeStruct((t * (d // LANES), LANES), F32)],
        compiler_params=_cparams(("parallel",)),
        name="merge",
    )(conv_o, ret_o, proj, proj, x2d, wc, wr, wo, ln_g, ln_b)


def _router_kernel(x_ref, w_ref, b_ref, tri_ref, idx_ref, gate_ref, rank_ref, cnt_ref, carry_ref):
    i = pl.program_id(0)

    @pl.when(i == 0)
    def _():
        carry_ref[...] = jnp.zeros_like(carry_ref)

    tb = x_ref.shape[0]
    n_e = w_ref.shape[1]
    logits = jnp.dot(x_ref[...], w_ref[...], preferred_element_type=F32,
                     precision=lax.Precision.HIGHEST) + b_ref[...]
    lane_e = lax.broadcasted_iota(jnp.int32, (tb, n_e), 1).astype(F32)
    work = logits
    vals, idxs = [], []
    for _ in range(TOP_K):
        m = jnp.max(work, axis=-1, keepdims=True)
        sel = jnp.min(jnp.where(work == m, lane_e, float(n_e)), axis=-1, keepdims=True)
        vals.append(m)
        idxs.append(sel)
        work = jnp.where(lane_e == sel, -jnp.inf, work)
    exps = [jnp.exp(v - vals[0]) for v in vals]
    denom = exps[0]
    for e in exps[1:]:
        denom = denom + e
    multihot = jnp.zeros((tb, n_e), F32)
    for sel in idxs:
        multihot = multihot + (lane_e == sel).astype(F32)
    before = jnp.dot(tri_ref[...], multihot.astype(BF16), preferred_element_type=F32) + carry_ref[...]
    carry_ref[...] = carry_ref[...] + jnp.sum(multihot, axis=0, keepdims=True)
    cnt_ref[...] = carry_ref[...].astype(jnp.int32)

    lane_o = lax.broadcasted_iota(jnp.int32, (tb, LANES), 1)
    idx_o = jnp.zeros((tb, LANES), jnp.int32)
    gate_o = jnp.zeros((tb, LANES), F32)
    rank_o = jnp.zeros((tb, LANES), jnp.int32)
    for k in range(TOP_K):
        rk = jnp.sum(jnp.where(lane_e == idxs[k], before, 0.0), axis=-1, keepdims=True).astype(jnp.int32)
        idx_o = jnp.where(lane_o == k, idxs[k].astype(jnp.int32), idx_o)
        gate_o = jnp.where(lane_o == k, exps[k] / denom, gate_o)
        rank_o = jnp.where(lane_o == k, rk, rank_o)
    idx_ref[...] = idx_o
    gate_ref[...] = gate_o
    rank_ref[...] = rank_o


def _router(x1, w_router, b_router, *, tb):
    t, d = x1.shape
    n_e = w_router.shape[1]
    tri = (jnp.arange(tb)[:, None] > jnp.arange(tb)[None, :]).astype(BF16)
    const = lambda i: (0, 0)
    row = lambda i: (i, 0)
    return pl.pallas_call(
        _router_kernel,
        grid=(t // tb,),
        in_specs=[
            pl.BlockSpec((tb, d), row),
            pl.BlockSpec((d, n_e), const),
            pl.BlockSpec((1, n_e), const),
            pl.BlockSpec((tb, tb), const),
        ],
        out_specs=[
            pl.BlockSpec((tb, LANES), row),
            pl.BlockSpec((tb, LANES), row),
            pl.BlockSpec((tb, LANES), row),
            pl.BlockSpec((1, n_e), const),
        ],
        out_shape=[
            jax.ShapeDtypeStruct((t, LANES), jnp.int32),
            jax.ShapeDtypeStruct((t, LANES), F32),
            jax.ShapeDtypeStruct((t, LANES), jnp.int32),
            jax.ShapeDtypeStruct((1, n_e), jnp.int32),
        ],
        scratch_shapes=[pltpu.VMEM((1, n_e), F32)],
        compiler_params=_cparams(("arbitrary",)),
        name="router",
    )(x1, w_router, b_router, tri)


def _dispatch_kernel(zstart_ref, zflag_ref, tail_ref, pos_hbm, x_ref, xs_hbm, pos_smem, zero_ref, sem_idx,
                     sem_row, *, tb, tm, zrows, sp):
    i = pl.program_id(0)
    n_e = zstart_ref.shape[0]

    def slab(ref, tok, n_tok):
        return ref.at[pl.ds(pl.multiple_of(tok * sp, sp), n_tok * sp)]

    @pl.when(i == 0)
    def _():
        zero_ref[...] = jnp.zeros_like(zero_ref)
        pieces = tm // zrows

        def zcopy(e, p):
            return pltpu.make_async_copy(zero_ref, slab(xs_hbm, zstart_ref[e] + p * zrows, zrows), sem_row)

        def start(e, c):
            @pl.when(zflag_ref[e] > 0)
            def _():
                for p in range(pieces):
                    zcopy(e, p).start()
            return c

        def wait(e, c):
            @pl.when(zflag_ref[e] > 0)
            def _():
                for p in range(pieces):
                    zcopy(e, p).wait()
            return c

        lax.fori_loop(0, n_e, start, 0)
        lax.fori_loop(0, n_e, wait, 0)

        def tail(blk, c):
            for p in range(pieces):
                cp = pltpu.make_async_copy(zero_ref, slab(xs_hbm, blk * tm + p * zrows, zrows), sem_row)
                cp.start()
                cp.wait()
            return c

        lax.fori_loop(tail_ref[0], xs_hbm.shape[0] // (sp * tm), tail, 0)

    idx_copy = pltpu.make_async_copy(pos_hbm.at[pl.ds(i * (tb * TOP_K), tb * TOP_K)], pos_smem, sem_idx)
    idx_copy.start()
    idx_copy.wait()

    def issue(r, c):
        src = x_ref.at[pl.ds(pl.multiple_of(r * sp, sp), sp)]
        for k in range(TOP_K):
            pltpu.make_async_copy(src, slab(xs_hbm, pos_smem[r * TOP_K + k], 1), sem_row).start(priority=k % 2)
        return c

    lax.fori_loop(0, tb, issue, 0, unroll=4)
    for k in range(TOP_K):
        pltpu.make_async_copy(x_ref, slab(xs_hbm, 0, tb), sem_row).wait()


def _dispatch(x1s, pos_flat, zstart, zflag, tail_blk, *, n_tok, n_pad, tb, tm, zrows):
    sp = x1s.shape[0] // n_tok
    return pl.pallas_call(
        functools.partial(_dispatch_kernel, tb=tb, tm=tm, zrows=zrows, sp=sp),
        grid_spec=pltpu.PrefetchScalarGridSpec(
            num_scalar_prefetch=3,
            grid=(n_tok // tb,),
            in_specs=[pl.BlockSpec(memory_space=pl.ANY),
                      pl.BlockSpec((tb * sp, LANES), lambda i, zs, zf, tl: (i, 0))],
            out_specs=pl.BlockSpec(memory_space=pl.ANY),
            scratch_shapes=[
                pltpu.SMEM((tb * TOP_K,), jnp.int32),
                pltpu.VMEM((zrows * sp, LANES), x1s.dtype),
                pltpu.SemaphoreType.DMA,
                pltpu.SemaphoreType.DMA,
            ],
        ),
        out_shape=jax.ShapeDtypeStruct((n_pad * sp, LANES), x1s.dtype),
        compiler_params=_cparams(("arbitrary",)),
        name="dispatch",
    )(zstart, zflag, tail_blk, pos_flat, x1s)


def _moe_kernel(be_ref, nv_ref, last_ref, x_ref, wg_ref, wu_ref, wd_ref, bg_ref, bu_ref, bd_ref, o_ref,
                xb_ref, acc_ref):
    b = pl.program_id(0)
    f = pl.program_id(1)
    n_f = pl.num_programs(1)

    @pl.when(nv_ref[b] > 0)
    def _():
        tm, d = xb_ref.shape
        s_per = d // LANES

        @pl.when(f == 0)
        def _():
            for s in range(s_per):
                xb_ref[:, s * LANES:(s + 1) * LANES] = _load_slab_cols(x_ref, s, tm, s_per).astype(BF16)

        xb = xb_ref[...]
        gate = jnp.dot(xb, wg_ref[...].astype(BF16), preferred_element_type=F32) + bg_ref[...]
        up = jnp.dot(xb, wu_ref[...].astype(BF16), preferred_element_type=F32) + bu_ref[...]
        gate = jnp.minimum(gate, SWIGLU_LIMIT)
        up = jnp.clip(up, -SWIGLU_LIMIT, SWIGLU_LIMIT)
        hdn = (up + 1.0) * (gate * _sigmoid(SWIGLU_ALPHA * gate))
        part = jnp.dot(hdn.astype(BF16), wd_ref[...].astype(BF16), preferred_element_type=F32)

        @pl.when(f == 0)
        def _():
            acc_ref[...] = part

        @pl.when(f > 0)
        def _():
            acc_ref[...] += part

        @pl.when(f == n_f - 1)
        def _():
            _store_slabs(o_ref, acc_ref[...] + bd_ref[...])

    @pl.when(jnp.logical_and(nv_ref[b] == 0, f == n_f - 1))
    def _():
        o_ref[...] = jnp.zeros_like(o_ref)


def _moe_ffn(xs, block_e, nvalid, last_blk, w_gate_up, b_gate_up, w_down, b_down, *, tm, tf):
    n_e, d, two_ff = w_gate_up.shape
    s_per = d // LANES
    n_pad = xs.shape[0] // s_per
    d_ff = two_ff // 2
    n_f = d_ff // tf
    n_blocks = n_pad // tm

    def blk(b, last):
        return jnp.minimum(b, last[0])

    def ftile(b, f, nv):
        return jnp.where(nv[b] > 0, f, n_f - 1)

    return pl.pallas_call(
        _moe_kernel,
        grid_spec=pltpu.PrefetchScalarGridSpec(
            num_scalar_prefetch=3,
            grid=(n_blocks, n_f),
            in_specs=[
                pl.BlockSpec((tm * s_per, LANES), lambda b, f, be, nv, last: (blk(b, last), 0)),
                pl.BlockSpec((None, d, tf), lambda b, f, be, nv, last: (be[b], 0, ftile(b, f, nv))),
                pl.BlockSpec((None, d, tf), lambda b, f, be, nv, last: (be[b], 0, n_f + ftile(b, f, nv))),
                pl.BlockSpec((None, tf, d), lambda b, f, be, nv, last: (be[b], ftile(b, f, nv), 0)),
                pl.BlockSpec((None, 1, tf), lambda b, f, be, nv, last: (be[b], 0, ftile(b, f, nv))),
                pl.BlockSpec((None, 1, tf), lambda b, f, be, nv, last: (be[b], 0, n_f + ftile(b, f, nv))),
                pl.BlockSpec((None, 1, d), lambda b, f, be, nv, last: (be[b], 0, 0)),
            ],
            out_specs=pl.BlockSpec((tm * s_per, LANES), lambda b, f, be, nv, last: (b, 0)),
            scratch_shapes=[pltpu.VMEM((tm, d), BF16), pltpu.VMEM((tm, d), F32)],
        ),
        out_shape=jax.ShapeDtypeStruct((n_pad * s_per, LANES), F32),
        compiler_params=_cparams(("arbitrary", "arbitrary")),
        name="moe_ffn",
    )(block_e, nvalid, last_blk, xs, w_gate_up, w_gate_up, w_down,
      b_gate_up.reshape(n_e, 1, two_ff), b_gate_up.reshape(n_e, 1, two_ff), b_down.reshape(n_e, 1, d))


def _combine_kernel(pos_hbm, yb_hbm, x_ref, gate_ref, lng_ref, lnb_ref, o_ref, pos_smem, buf_ref, ffn_ref,
                    sem_idx, sem_row, *, tb, alpha):
    i = pl.program_id(0)
    s_per = x_ref.shape[1] // LANES
    idx_copy = pltpu.make_async_copy(pos_hbm.at[pl.ds(i * (tb * TOP_K), tb * TOP_K)], pos_smem, sem_idx)
    idx_copy.start()
    idx_copy.wait()

    def issue(r, c):
        for k in range(TOP_K):
            src = pl.multiple_of(pos_smem[r * TOP_K + k] * s_per, s_per)
            dst = pl.multiple_of(r * s_per, s_per)
            pltpu.make_async_copy(yb_hbm.at[pl.ds(src, s_per)], buf_ref.at[k, pl.ds(dst, s_per)],
                                  sem_row).start(priority=k % 2)
        return c

    lax.fori_loop(0, tb, issue, 0, unroll=4)
    for k in range(TOP_K):
        pltpu.make_async_copy(yb_hbm.at[pl.ds(0, tb * s_per)], buf_ref.at[k], sem_row).wait()

    for s in range(s_per):
        piece = gate_ref[:, 0:1] * _load_slab_cols(buf_ref.at[0], s, tb, s_per)
        for k in range(1, TOP_K):
            piece = piece + gate_ref[:, k:k + 1] * _load_slab_cols(buf_ref.at[k], s, tb, s_per)
        ffn_ref[:, s * LANES:(s + 1) * LANES] = piece
    o_ref[...] = _layernorm(alpha * x_ref[...] + ffn_ref[...], lng_ref[...], lnb_ref[...])


def _combine(pos_flat, yb, x1, gates, ln_g, ln_b, *, tb, alpha):
    t, d = x1.shape
    s_per = d // LANES
    const = lambda i: (0, 0)
    row = lambda i: (i, 0)
    return pl.pallas_call(
        functools.partial(_combine_kernel, tb=tb, alpha=alpha),
        grid=(t // tb,),
        in_specs=[
            pl.BlockSpec(memory_space=pl.ANY),
            pl.BlockSpec(memory_space=pl.ANY),
            pl.BlockSpec((tb, d), row),
            pl.BlockSpec((tb, LANES), row),
            pl.BlockSpec((1, d), const),
            pl.BlockSpec((1, d), const),
        ],
        out_specs=pl.BlockSpec((tb, d), row),
        out_shape=jax.ShapeDtypeStruct((t, d), F32),
        scratch_shapes=[
            pltpu.SMEM((tb * TOP_K,), jnp.int32),
            pltpu.VMEM((TOP_K, tb * s_per, LANES), yb.dtype),
            pltpu.VMEM((tb, d), F32),
            pltpu.SemaphoreType.DMA,
            pltpu.SemaphoreType.DMA,
        ],
        compiler_params=_cparams(("arbitrary",)),
        name="combine",
    )(pos_flat, yb, x1, gates, ln_g, ln_b)


def _tiles(batch, seq):
    t = batch * seq
    return dict(
        proj_tm=min(1024, seq), proj_tn=1024,
        conv_tc=min(256, seq),
        merge_tm=min(256, t),
        router_tb=min(512, t),
        dispatch_tb=min(512, t // TOP_K),
        moe_tm=min(512, max(128, t // 8)), moe_tf=512,
        combine_tb=min(256, t),
    )


def _layer(x2d, p, *, batch, seq, depth, cfg):
    t, d = x2d.shape
    alpha = (2 * depth) ** 0.25
    q_w = RET_HEADS * RET_DK
    v_w = RET_HEADS * RET_DV
    ch = p["w_dw"].shape[1]
    n_e = p["w_router"].shape[1]
    row2 = lambda a: a.reshape(1, -1)

    cos_t, sin_t = _rope_tables(seq)
    proj = _in_proj(x2d, p["w_in"].astype(BF16), cos_t, sin_t, seq=seq,
                    tm=cfg["proj_tm"], tn=cfg["proj_tn"], rope_cols=2 * q_w)
    ret_o = _retention(proj, _retention_tables(), row2(p["ret_gn_g"]), row2(p["ret_gn_b"]),
                       batch=batch, seq=seq)
    conv_o = _conv_branch(proj, p["w_dw"], row2(p["b_dw"]), row2(p["conv_ln_g"]), row2(p["conv_ln_b"]),
                          batch=batch, seq=seq, tc=cfg["conv_tc"], col0=2 * q_w + 2 * v_w)
    x1, x1s = _merge(conv_o, ret_o, proj, x2d, p["w_conv_out"].astype(BF16), p["w_ret_out"].astype(BF16),
                     p["w_o"].astype(BF16), row2(p["ln1_g"]), row2(p["ln1_b"]),
                     tm=cfg["merge_tm"], gate_col0=2 * q_w + 2 * v_w + 2 * ch, alpha=alpha)

    idx_o, gate_o, rank_o, counts = _router(x1, p["w_router"], row2(p["b_router"]), tb=cfg["router_tb"])

    tm = cfg["moe_tm"]
    counts = counts[0]
    padded = ((counts + tm - 1) // tm) * tm
    pad_end = jnp.cumsum(padded)
    pad_start = pad_end - padded
    pos = pad_start[idx_o[:, :TOP_K]] + rank_o[:, :TOP_K]
    pos_flat = pos.reshape(-1).astype(jnp.int32)
    n_pad = t * TOP_K + n_e * tm
    n_blocks = n_pad // tm
    blk_row0 = jnp.arange(n_blocks, dtype=jnp.int32) * tm
    block_e = jnp.minimum(jnp.sum(pad_end[None, :] <= blk_row0[:, None], axis=1), n_e - 1).astype(jnp.int32)
    nvalid = jnp.clip(counts[block_e] - (blk_row0 - pad_start[block_e]), 0, tm).astype(jnp.int32)
    last_blk = (pad_end[-1:] // tm - 1).astype(jnp.int32)
    zstart = (pad_end - tm).astype(jnp.int32)
    zflag = (padded > 0).astype(jnp.int32)

    xs = _dispatch(x1s, pos_flat, zstart, zflag, last_blk + 1, n_tok=t, n_pad=n_pad, tb=cfg["dispatch_tb"],
                   tm=tm, zrows=min(256, tm))
    yb = _moe_ffn(xs, block_e, nvalid, last_blk, p["w_gate_up"], p["b_gate_up"], p["w_down"], p["b_down"],
                  tm=tm, tf=cfg["moe_tf"])
    return _combine(pos_flat, yb, x1, gate_o, row2(p["ln2_g"]), row2(p["ln2_b"]),
                    tb=cfg["combine_tb"], alpha=alpha)


_PARAM_NAMES = ("w_in", "w_dw", "b_dw", "conv_ln_g", "conv_ln_b", "w_conv_out", "ret_gn_g", "ret_gn_b",
                "w_ret_out", "w_o", "ln1_g", "ln1_b", "w_router", "b_router", "w_gate_up", "b_gate_up",
                "w_down", "b_down", "ln2_g", "ln2_b")


def _forward(x, params, cfg=None):
    batch, seq, d = x.shape
    depth = params[0].shape[0]
    cfg = cfg or _tiles(batch, seq)
    x2d = x.reshape(batch * seq, d)
    for l in range(depth):
        p = {name: w[l] for name, w in zip(_PARAM_NAMES, params)}
        x2d = _layer(x2d, p, batch=batch, seq=seq, depth=depth, cfg=cfg)
    return x2d.reshape(batch, seq, d)


def kernel(x, w_in, w_dw, b_dw, conv_ln_g, conv_ln_b, w_conv_out, ret_gn_g, ret_gn_b, w_ret_out, w_o,
           ln1_g, ln1_b, w_router, b_router, w_gate_up, b_gate_up, w_down, b_down, ln2_g, ln2_b):
    return _forward(x, (w_in, w_dw, b_dw, conv_ln_g, conv_ln_b, w_conv_out, ret_gn_g, ret_gn_b, w_ret_out,
                        w_o, ln1_g, ln1_b, w_router, b_router, w_gate_up, b_gate_up, w_down, b_down,
                        ln2_g, ln2_b))
```
